```python
import math
import jax, jax.numpy as jnp
from jax import lax
import numpy as np

D_MODEL = 2048
BATCH = 2
SEQ = 16384
DEPTH = 2

N_EVEN = (DEPTH + 1) // 2
N_ODD = DEPTH // 2

POOL_WIDTH = D_MODEL // 2
POOL_WINDOWS = (2, 4, 8, 16)
N_POOL_GROUPS = len(POOL_WINDOWS)
POOL_GROUP = POOL_WIDTH // N_POOL_GROUPS

CONV_WIDTH = D_MODEL - POOL_WIDTH
CONV_K = 3
IN_PROJ_WIDTH = POOL_WIDTH + 3 * CONV_WIDTH

N_HEADS = 16
HEAD_DIM = D_MODEL // N_HEADS
Q_BLOCK = 128
FOX_IN_WIDTH = 3 * D_MODEL + N_HEADS

N_EXPERTS = 32
TOP_K = 4
D_FF = D_MODEL
SWIGLU_ALPHA = 1.702
SWIGLU_LIMIT = 7.0
MOE_BLOCK = 256

LN_EPS = 1e-5
DEEPNORM_ALPHA = (2.0 * DEPTH) ** 0.25
DEEPNORM_BETA = (8.0 * DEPTH) ** -0.25

kernel_name = "hybrid_pool_conv_fox_moe_deepnorm"


def layer_norm(x, g, b):
    xf = x.astype(jnp.float32)
    mu = jnp.mean(xf, axis=-1, keepdims=True)
    xc = xf - mu
    var = jnp.mean(xc * xc, axis=-1, keepdims=True)
    y = xc * lax.rsqrt(var + LN_EPS)
    return (y * g.astype(jnp.float32) + b.astype(jnp.float32)).astype(x.dtype)


def pool_mixer(a, pool_w, pool_scale):
    bsz, s_len, _ = a.shape
    af = a.astype(jnp.float32)
    cs = jnp.pad(jnp.cumsum(af, axis=1), ((0, 0), (1, 0), (0, 0)))
    t = jnp.arange(s_len)
    outs = []
    for g, w in enumerate(POOL_WINDOWS):
        lo_c, hi_c = g * POOL_GROUP, (g + 1) * POOL_GROUP
        cs_g = cs[:, :, lo_c:hi_c]
        upper = cs_g[:, 1:, :]
        lower = jnp.pad(cs_g[:, : s_len + 1 - w, :], ((0, 0), (w - 1, 0), (0, 0)))
        count = jnp.minimum(t + 1, w).astype(jnp.float32)[None, :, None]
        outs.append((upper - lower) / count - af[:, :, lo_c:hi_c])
    p = jnp.stack(outs, axis=2).astype(a.dtype)
    y = jnp.einsum('bsgc,gce->bsge', p, pool_w).reshape(bsz, s_len, POOL_WIDTH)
    return y * pool_scale


def short_conv_mixer(gate_b, gate_c, v, conv_w):
    z = gate_c * v
    y = lax.conv_general_dilated(
        z, conv_w[:, None, :], window_strides=(1,), padding=[(CONV_K - 1, 0)],
        dimension_numbers=('NWC', 'WIO', 'NWC'), feature_group_count=CONV_WIDTH)
    return gate_b * y


def pool_conv_layer_mixer(x, w_in, pool_w, pool_scale, conv_w, w_out):
    h = jnp.einsum('bsd,de->bse', x, w_in)
    a = h[..., :POOL_WIDTH]
    gate_b = h[..., POOL_WIDTH:POOL_WIDTH + CONV_WIDTH]
    gate_c = h[..., POOL_WIDTH + CONV_WIDTH:POOL_WIDTH + 2 * CONV_WIDTH]
    v = h[..., POOL_WIDTH + 2 * CONV_WIDTH:]
    y = jnp.concatenate([pool_mixer(a, pool_w, pool_scale),
                         short_conv_mixer(gate_b, gate_c, v, conv_w)], axis=-1)
    return jnp.einsum('bse,ed->bsd', y, w_out)


def forgetting_attention(x, w_in, b_f, w_o):
    bsz, s_len, _ = x.shape
    n_blocks = s_len // Q_BLOCK
    h = jnp.einsum('bsd,de->bse', x, w_in)
    q = h[..., :D_MODEL].reshape(bsz, s_len, N_HEADS, HEAD_DIM) * (HEAD_DIM ** -0.5)
    k = h[..., D_MODEL:2 * D_MODEL].reshape(bsz, s_len, N_HEADS, HEAD_DIM)
    v = h[..., 2 * D_MODEL:3 * D_MODEL].reshape(bsz, s_len, N_HEADS, HEAD_DIM)
    log_f = jax.nn.log_sigmoid((h[..., 3 * D_MODEL:] + b_f).astype(jnp.float32))
    c = jnp.cumsum(log_f, axis=1).transpose(0, 2, 1)
    q_blocks = q.reshape(bsz, n_blocks, Q_BLOCK, N_HEADS, HEAD_DIM).transpose(1, 0, 2, 3, 4)
    c_blocks = c.reshape(bsz, N_HEADS, n_blocks, Q_BLOCK).transpose(2, 0, 1, 3)
    k_pos = jnp.arange(s_len)

    def attend_block(args):
        q_blk, c_q, blk = args
        q_pos = blk * Q_BLOCK + jnp.arange(Q_BLOCK)
        logits = jnp.einsum('bqhd,bkhd->bhqk', q_blk, k).astype(jnp.float32)
        logits = logits + c_q[..., :, None] - c[:, :, None, :]
        causal = k_pos[None, :] <= q_pos[:, None]
        logits = jnp.where(causal, logits, -jnp.inf)
        probs = jax.nn.softmax(logits, axis=-1).astype(v.dtype)
        return jnp.einsum('bhqk,bkhd->bqhd', probs, v)

    o = lax.map(attend_block, (q_blocks, c_blocks, jnp.arange(n_blocks)))
    o = o.transpose(1, 0, 2, 3, 4).reshape(bsz, s_len, D_MODEL)
    return jnp.einsum('bse,ed->bsd', o, w_o)


def moe_ffn(x, router_w, router_b, w_gu, b_gu, w_down, b_down):
    bsz, s_len, d = x.shape
    n_tok = bsz * s_len
    xt = x.reshape(n_tok, d)
    logits = (jnp.einsum('td,de->te', xt, router_w) + router_b).astype(jnp.float32)
    top_v, top_e = lax.top_k(logits, TOP_K)
    gates = jax.nn.softmax(top_v, axis=-1)

    n_assign = n_tok * TOP_K
    flat_e = top_e.reshape(-1)
    flat_tok = jnp.arange(n_assign, dtype=jnp.int32) // TOP_K
    order = jnp.argsort(flat_e)
    e_sorted = flat_e[order]
    counts = jnp.bincount(flat_e, length=N_EXPERTS)
    padded = (counts + MOE_BLOCK - 1) // MOE_BLOCK * MOE_BLOCK
    pad_end = jnp.cumsum(padded)
    pad_start = pad_end - padded
    start = jnp.cumsum(counts) - counts
    dest = pad_start[e_sorted] + jnp.arange(n_assign, dtype=jnp.int32) - start[e_sorted]
    n_blocks = -(-(n_assign + N_EXPERTS * (MOE_BLOCK - 1)) // MOE_BLOCK)
    n_slots = n_blocks * MOE_BLOCK
    slot_tok = jnp.full((n_slots,), n_tok, jnp.int32).at[dest].set(flat_tok[order])
    slot_gate = jnp.zeros((n_slots,), jnp.float32).at[dest].set(gates.reshape(-1)[order])
    block_e = jnp.minimum(
        jnp.searchsorted(pad_end, jnp.arange(n_blocks, dtype=jnp.int32) * MOE_BLOCK, side='right'),
        N_EXPERTS - 1)
    x_pad = jnp.concatenate([xt, jnp.zeros((1, d), xt.dtype)], axis=0)

    def expert_block(args):
        tok, e = args
        xb = x_pad[tok]
        gu = xb @ w_gu[e] + b_gu[e]
        x_glu = jnp.minimum(gu[:, :D_FF], SWIGLU_LIMIT)
        x_lin = jnp.clip(gu[:, D_FF:], -SWIGLU_LIMIT, SWIGLU_LIMIT)
        hid = (x_lin + 1.0) * (x_glu * jax.nn.sigmoid(SWIGLU_ALPHA * x_glu))
        return hid @ w_down[e] + b_down[e]

    yb = lax.map(expert_block, (slot_tok.reshape(n_blocks, MOE_BLOCK), block_e))
    y = yb.reshape(n_slots, d) * slot_gate[:, None].astype(yb.dtype)
    out = jnp.zeros((n_tok + 1, d), y.dtype).at[slot_tok].add(y)[:n_tok]
    return out.reshape(bsz, s_len, d)


def setup_inputs(seed: int = 0) -> dict:
    key = jax.random.key(seed)
    ks = jax.random.split(key, 24)
    nrm = jax.random.normal
    f32 = jnp.float32
    d = D_MODEL
    return {
        "x": nrm(ks[0], (BATCH, SEQ, d), f32),
        "ab_w_in": nrm(ks[1], (N_EVEN, d, IN_PROJ_WIDTH), f32) * d ** -0.5,
        "ab_pool_w": nrm(ks[2], (N_EVEN, N_POOL_GROUPS, POOL_GROUP, POOL_GROUP), f32) * POOL_GROUP ** -0.5,
        "ab_pool_scale": 1.0 + 0.1 * nrm(ks[3], (N_EVEN, POOL_WIDTH), f32),
        "ab_conv_w": nrm(ks[4], (N_EVEN, CONV_K, CONV_WIDTH), f32) * CONV_K ** -0.5,
        "ab_w_out": nrm(ks[5], (N_EVEN, POOL_WIDTH + CONV_WIDTH, d), f32) * (POOL_WIDTH + CONV_WIDTH) ** -0.5 * DEEPNORM_BETA,
        "fox_w_in": nrm(ks[6], (N_ODD, d, FOX_IN_WIDTH), f32) * d ** -0.5,
        "fox_b_f": jnp.linspace(1.0, 6.0, N_HEADS, dtype=f32)[None, :] + 0.1 * nrm(ks[7], (N_ODD, N_HEADS), f32),
        "fox_w_o": nrm(ks[8], (N_ODD, d, d), f32) * d ** -0.5 * DEEPNORM_BETA,
        "ln1_g": 1.0 + 0.05 * nrm(ks[9], (DEPTH, d), f32),
        "ln1_b": 0.02 * nrm(ks[10], (DEPTH, d), f32),
        "ln2_g": 1.0 + 0.05 * nrm(ks[11], (DEPTH, d), f32),
        "ln2_b": 0.02 * nrm(ks[12], (DEPTH, d), f32),
        "router_w": nrm(ks[13], (DEPTH, d, N_EXPERTS), f32) * d ** -0.5,
        "router_b": 0.01 * nrm(ks[14], (DEPTH, N_EXPERTS), f32),
        "w_gu": nrm(ks[15], (DEPTH, N_EXPERTS, d, 2 * D_FF), f32) * d ** -0.5,
        "b_gu": 0.01 * nrm(ks[16], (DEPTH, N_EXPERTS, 2 * D_FF), f32),
        "w_down": nrm(ks[17], (DEPTH, N_EXPERTS, D_FF, d), f32) * D_FF ** -0.5 * DEEPNORM_BETA,
        "b_down": 0.01 * nrm(ks[18], (DEPTH, N_EXPERTS, d), f32),
    }


def reference(x, ab_w_in, ab_pool_w, ab_pool_scale, ab_conv_w, ab_w_out,
              fox_w_in, fox_b_f, fox_w_o,
              ln1_g, ln1_b, ln2_g, ln2_b,
              router_w, router_b, w_gu, b_gu, w_down, b_down):
    for layer in range(DEPTH):
        j = layer // 2
        if layer % 2 == 0:
            mix = pool_conv_layer_mixer(x, ab_w_in[j], ab_pool_w[j], ab_pool_scale[j],
                                        ab_conv_w[j], ab_w_out[j])
        else:
            mix = forgetting_attention(x, fox_w_in[j], fox_b_f[j], fox_w_o[j])
        x = layer_norm(DEEPNORM_ALPHA * x + mix, ln1_g[layer], ln1_b[layer])
        ffn = moe_ffn(x, router_w[layer], router_b[layer], w_gu[layer], b_gu[layer],
                      w_down[layer], b_down[layer])
        x = layer_norm(DEEPNORM_ALPHA * x + ffn, ln2_g[layer], ln2_b[layer])
    return x
```

```python
import functools

import jax
import jax.numpy as jnp
from jax import lax
from jax.experimental import pallas as pl
from jax.experimental.pallas import tpu as pltpu

F32 = jnp.float32
BF16 = jnp.bfloat16
I32 = jnp.int32
HIGHEST = lax.Precision.HIGHEST

LN_EPS = 1e-5
POOL_WINDOWS = (2, 4, 8, 16)
CONV_K = 3
HALO = 16
TOP_K = 4
SWIGLU_ALPHA = 1.702
SWIGLU_LIMIT = 7.0
MOE_TM = 512
VMEM_LIMIT = 56 * 2**20


def _cparams(n_axes):
    return pltpu.CompilerParams(dimension_semantics=("arbitrary",) * n_axes, vmem_limit_bytes=VMEM_LIMIT)


def _tile(n, pref):
    t = min(n, pref)
    while n % t:
        t -= 128
    assert t > 0, (n, pref)
    return t


def _layer_norm(z, g, b):
    mu = jnp.mean(z, axis=-1, keepdims=True)
    zc = z - mu
    var = jnp.mean(zc * zc, axis=-1, keepdims=True)
    return zc * lax.rsqrt(var + LN_EPS) * g + b


def _mm_body(a_ref, b_ref, s_ref, o_ref, a_bf):
    @pl.when(pl.program_id(1) == 0)
    def _():
        a_bf[...] = a_ref[...].astype(BF16)

    acc = jnp.dot(a_bf[...], b_ref[...], preferred_element_type=F32)
    o_ref[...] = (acc * s_ref[...]).astype(o_ref.dtype)


def _matmul(a, b, col_scale, out_dtype, tm=512, tn=1024):
    m, k = a.shape
    n = b.shape[1]
    tm, tn = _tile(m, tm), _tile(n, tn)
    return pl.pallas_call(
        _mm_body,
        grid=(m // tm, n // tn),
        in_specs=[pl.BlockSpec((tm, k), lambda i, j: (i, 0)),
                  pl.BlockSpec((k, tn), lambda i, j: (0, j)),
                  pl.BlockSpec((1, tn), lambda i, j: (0, j))],
        out_specs=pl.BlockSpec((tm, tn), lambda i, j: (i, j)),
        out_shape=jax.ShapeDtypeStruct((m, n), out_dtype),
        scratch_shapes=[pltpu.VMEM((tm, k), BF16)],
        compiler_params=_cparams(2),
        name="dense_matmul",
    )(a, b, col_scale)


def _mixer0_body(h_ref, halo_ref, x_ref, pw_ref, ps_ref, cw_ref, wo_ref, g_ref, b_ref, o_ref,
                 *, tm, tiles_per_seq, pool_w, conv_w, alpha):
    t_in_seq = pl.program_id(0) % tiles_per_seq
    first = t_in_seq == 0
    group = pool_w // len(POOL_WINDOWS)
    c0, c1, c2 = pool_w, pool_w + conv_w, pool_w + 2 * conv_w

    a = h_ref[:, 0:pool_w]
    a_halo = jnp.where(first, 0.0, halo_ref[:, 0:pool_w])
    s = jnp.concatenate([a_halo, a], axis=0)
    pos = t_in_seq * tm + lax.broadcasted_iota(I32, (tm, 1), 0)
    pooled = []
    for gi, w in enumerate(POOL_WINDOWS):
        s = s + pltpu.roll(s, w // 2, 0)
        inv_count = 1.0 / jnp.minimum(pos + 1, w).astype(F32)
        p = s[HALO:, 0:group] * inv_count - a[:, gi * group:(gi + 1) * group]
        pooled.append(jnp.dot(p.astype(BF16), pw_ref[gi], preferred_element_type=F32))
        if gi + 1 < len(POOL_WINDOWS):
            s = s[:, group:]
    y_pool = jnp.concatenate(pooled, axis=1) * ps_ref[...]

    z = h_ref[:, c1:c2] * h_ref[:, c2:]
    z_halo = jnp.where(first, 0.0, halo_ref[:, c1:c2] * halo_ref[:, c2:])
    z_ext = jnp.concatenate([z_halo, z], axis=0)
    conv = (cw_ref[2:3, :] * z_ext + cw_ref[1:2, :] * pltpu.roll(z_ext, 1, 0)
            + cw_ref[0:1, :] * pltpu.roll(z_ext, 2, 0))
    y_conv = h_ref[:, c0:c1] * conv[HALO:]

    y = jnp.concatenate([y_pool, y_conv], axis=1).astype(BF16)
    mix = jnp.dot(y, wo_ref[...], preferred_element_type=F32)
    o_ref[...] = _layer_norm(alpha * x_ref[...] + mix, g_ref[...], b_ref[...])


def _mixer0(h, x, pool_w, pool_scale, conv_w, w_out, g, b, *, seq, alpha, tm=256):
    t, d = x.shape
    width = h.shape[1]
    pw = pool_scale.shape[1]
    cw = conv_w.shape[1]
    tm = _tile(seq, tm)
    body = functools.partial(_mixer0_body, tm=tm, tiles_per_seq=seq // tm, pool_w=pw, conv_w=cw, alpha=alpha)
    full = lambda shape: pl.BlockSpec(shape, lambda i: (0,) * len(shape))
    return pl.pallas_call(
        body,
        grid=(t // tm,),
        in_specs=[pl.BlockSpec((tm, width), lambda i: (i, 0)),
                  pl.BlockSpec((HALO, width), lambda i: (jnp.maximum(i * (tm // HALO) - 1, 0), 0)),
                  pl.BlockSpec((tm, d), lambda i: (i, 0)),
                  full(pool_w.shape), full(pool_scale.shape), full(conv_w.shape), full(w_out.shape),
                  full(g.shape), full(b.shape)],
        out_specs=pl.BlockSpec((tm, d), lambda i: (i, 0)),
        out_shape=jax.ShapeDtypeStruct((t, d), F32),
        compiler_params=_cparams(1),
        name="pool_conv_mixer",
    )(h, h, x, pool_w, pool_scale, conv_w, w_out, g, b)


def _proj_ln_body(a_ref, w_ref, x_ref, g_ref, b_ref, o_ref, *, alpha):
    mix = jnp.dot(a_ref[...], w_ref[...], preferred_element_type=F32)
    o_ref[...] = _layer_norm(alpha * x_ref[...] + mix, g_ref[...], b_ref[...])


def _proj_ln(a, w, x, g, b, *, alpha, tm=512):
    t, d = x.shape
    tm = _tile(t, tm)
    full = lambda shape: pl.BlockSpec(shape, lambda i: (0,) * len(shape))
    return pl.pallas_call(
        functools.partial(_proj_ln_body, alpha=alpha),
        grid=(t // tm,),
        in_specs=[pl.BlockSpec((tm, a.shape[1]), lambda i: (i, 0)), full(w.shape),
                  pl.BlockSpec((tm, d), lambda i: (i, 0)), full(g.shape), full(b.shape)],
        out_specs=pl.BlockSpec((tm, d), lambda i: (i, 0)),
        out_shape=jax.ShapeDtypeStruct((t, d), F32),
        compiler_params=_cparams(1),
        name="proj_residual_ln",
    )(a, w, x, g, b)


def _select_columns(cols, tm):
    lane = lax.broadcasted_iota(I32, (tm, len(cols)), 1)
    out = jnp.broadcast_to(cols[-1], (tm, len(cols)))
    for k in range(len(cols) - 2, -1, -1):
        out = jnp.where(lane == k, cols[k], out)
    return out


def _router_body(x_ref, rw_ref, rb_ref, te_ref, gt_ref, rk_ref, cnt_ref, carry, *, tm, n_exp):
    @pl.when(pl.program_id(0) == 0)
    def _():
        carry[...] = jnp.zeros_like(carry)

    logits = jnp.dot(x_ref[...], rw_ref[...], precision=HIGHEST, preferred_element_type=F32) + rb_ref[...]
    lane = lax.broadcasted_iota(I32, (tm, n_exp), 1)
    vals, idxs, hots = [], [], []
    rest = logits
    for _ in range(TOP_K):
        m = jnp.max(rest, axis=-1, keepdims=True)
        idx = jnp.min(jnp.where(rest == m, lane, n_exp), axis=-1, keepdims=True)
        hot = lane == idx
        rest = jnp.where(hot, -jnp.inf, rest)
        vals.append(m)
        idxs.append(idx)
        hots.append(hot)
    exps = [jnp.exp(v - vals[0]) for v in vals]
    inv_sum = 1.0 / sum(exps)
    gates = [e * inv_sum for e in exps]

    cnt = sum(h.astype(F32) for h in hots)
    row = lax.broadcasted_iota(I32, (tm, tm), 0)
    col = lax.broadcasted_iota(I32, (tm, tm), 1)
    earlier = (col < row).astype(BF16)
    before = jnp.dot(earlier, cnt.astype(BF16), preferred_element_type=F32) + carry[...]
    ranks = [jnp.sum(jnp.where(h, before, 0.0), axis=-1, keepdims=True).astype(I32) for h in hots]

    te_ref[...] = _select_columns(idxs, tm)
    gt_ref[...] = _select_columns(gates, tm)
    rk_ref[...] = _select_columns(ranks, tm)
    carry[...] = carry[...] + jnp.sum(cnt, axis=0, keepdims=True)
    cnt_ref[...] = carry[...]


def _router(x, rw, rb, tm=512):
    t, d = x.shape
    n_exp = rw.shape[1]
    tm = _tile(t, tm)
    small = lambda: pl.BlockSpec((tm, TOP_K), lambda i: (i, 0))
    return pl.pallas_call(
        functools.partial(_router_body, tm=tm, n_exp=n_exp),
        grid=(t // tm,),
        in_specs=[pl.BlockSpec((tm, d), lambda i: (i, 0)),
                  pl.BlockSpec((d, n_exp), lambda i: (0, 0)),
                  pl.BlockSpec((1, n_exp), lambda i: (0, 0))],
        out_specs=[small(), small(), small(), pl.BlockSpec((1, n_exp), lambda i: (0, 0))],
        out_shape=[jax.ShapeDtypeStruct((t, TOP_K), I32), jax.ShapeDtypeStruct((t, TOP_K), F32),
                   jax.ShapeDtypeStruct((t, TOP_K), I32), jax.ShapeDtypeStruct((1, n_exp), F32)],
        scratch_shapes=[pltpu.VMEM((1, n_exp), F32)],
        compiler_params=_cparams(1),
        name="moe_router",
    )(x, rw, rb)


def _dispatch_body(pe_ref, pd_ref, dest_ref, x_ref, xs_ref, zbuf, sem, zsem, *, tm, n_exp, n_blocks):
    def zero_block(start):
        return pltpu.make_async_copy(zbuf, xs_ref.at[pl.ds(pl.multiple_of(start, MOE_TM), MOE_TM), :], zsem)

    @pl.when(pl.program_id(0) == 0)
    def _():
        zbuf[...] = jnp.zeros_like(zbuf)
        n_used = pe_ref[n_exp - 1] // MOE_TM
        for e in range(n_exp):
            @pl.when(pd_ref[e] > 0)
            def _():
                zero_block(pe_ref[e] - MOE_TM).start()
        lax.fori_loop(n_used, n_blocks, lambda bi, c: (zero_block(bi * MOE_TM).start(), c)[1], 0)
        for e in range(n_exp):
            @pl.when(pd_ref[e] > 0)
            def _():
                zero_block(pe_ref[e] - MOE_TM).wait()
        lax.fori_loop(n_used, n_blocks, lambda bi, c: (zero_block(bi * MOE_TM).wait(), c)[1], 0)

    def issue(r, _):
        for k in range(TOP_K):
            slot = dest_ref[r * TOP_K + k]
            pltpu.make_async_copy(x_ref.at[pl.ds(r, 1), :], xs_ref.at[pl.ds(slot, 1), :], sem).start()
        return 0

    lax.fori_loop(0, tm, issue, 0, unroll=8)
    for _ in range(TOP_K):
        pltpu.make_async_copy(x_ref, xs_ref.at[pl.ds(0, tm), :], sem).wait()


def _dispatch(x, dest_flat, pad_end, padded, n_slots, tm=256):
    t, d = x.shape
    tm = _tile(t, tm)
    n_exp = pad_end.shape[0]
    grid_spec = pltpu.PrefetchScalarGridSpec(
        num_scalar_prefetch=2,
        grid=(t // tm,),
        in_specs=[pl.BlockSpec((tm * TOP_K,), lambda i, pe, pd: (i,), memory_space=pltpu.SMEM),
                  pl.BlockSpec((tm, d), lambda i, pe, pd: (i, 0))],
        out_specs=pl.BlockSpec(memory_space=pl.ANY),
        scratch_shapes=[pltpu.VMEM((MOE_TM, d), F32), pltpu.SemaphoreType.DMA, pltpu.SemaphoreType.DMA],
    )
    return pl.pallas_call(
        functools.partial(_dispatch_body, tm=tm, n_exp=n_exp, n_blocks=n_slots // MOE_TM),
        grid_spec=grid_spec,
        out_shape=jax.ShapeDtypeStruct((n_slots, d), F32),
        compiler_params=_cparams(1),
        name="moe_dispatch",
    )(pad_end, padded, dest_flat, x)


def _experts_body(be_ref, nv_ref, x_ref, wg_ref, wl_ref, bg_ref, bl_ref, wd_ref, bd_ref, o_ref, xb):
    i, j = pl.program_id(0), pl.program_id(1)

    @pl.when(i < nv_ref[0])
    def _():
        @pl.when(j == 0)
        def _():
            xb[...] = x_ref[...].astype(BF16)

        x = xb[...]
        glu = jnp.dot(x, wg_ref[...], preferred_element_type=F32) + bg_ref[...]
        lin = jnp.dot(x, wl_ref[...], preferred_element_type=F32) + bl_ref[...]
        glu = jnp.minimum(glu, SWIGLU_LIMIT)
        lin = jnp.clip(lin, -SWIGLU_LIMIT, SWIGLU_LIMIT)
        hid = (lin + 1.0) * (glu * jax.nn.sigmoid(SWIGLU_ALPHA * glu))
        part = jnp.dot(hid.astype(BF16), wd_ref[...], preferred_element_type=F32)

        @pl.when(j == 0)
        def _():
            o_ref[...] = part + bd_ref[...]

        @pl.when(j > 0)
        def _():
            o_ref[...] += part

    @pl.when(jnp.logical_and(i >= nv_ref[0], j == 0))
    def _():
        o_ref[...] = jnp.zeros_like(o_ref)


def _experts(xs, block_e, n_valid, w_gu, b_gu, w_down, b_down, tf=512):
    n_slots, d = xs.shape
    n_exp, _, f2 = w_gu.shape
    f = f2 // 2
    tf = _tile(f, tf)
    nf = f // tf
    nb = n_slots // MOE_TM

    def blk(i, nv):
        return jnp.minimum(i, nv[0] - 1)

    def fch(i, j, nv):
        return jnp.where(i < nv[0], j, nf - 1)

    grid_spec = pltpu.PrefetchScalarGridSpec(
        num_scalar_prefetch=2,
        grid=(nb, nf),
        in_specs=[pl.BlockSpec((MOE_TM, d), lambda i, j, be, nv: (blk(i, nv), 0)),
                  pl.BlockSpec((None, d, tf), lambda i, j, be, nv: (be[blk(i, nv)], 0, fch(i, j, nv))),
                  pl.BlockSpec((None, d, tf), lambda i, j, be, nv: (be[blk(i, nv)], 0, nf + fch(i, j, nv))),
                  pl.BlockSpec((None, 1, tf), lambda i, j, be, nv: (be[blk(i, nv)], 0, fch(i, j, nv))),
                  pl.BlockSpec((None, 1, tf), lambda i, j, be, nv: (be[blk(i, nv)], 0, nf + fch(i, j, nv))),
                  pl.BlockSpec((None, tf, d), lambda i, j, be, nv: (be[blk(i, nv)], fch(i, j, nv), 0)),
                  pl.BlockSpec((None, 1, d), lambda i, j, be, nv: (be[blk(i, nv)], 0, 0))],
        out_specs=pl.BlockSpec((MOE_TM, d), lambda i, j, be, nv: (i, 0)),
        scratch_shapes=[pltpu.VMEM((MOE_TM, d), BF16)],
    )
    return pl.pallas_call(
        _experts_body,
        grid_spec=grid_spec,
        out_shape=jax.ShapeDtypeStruct((n_slots, d), F32),
        compiler_params=_cparams(2),
        name="moe_experts",
    )(block_e, n_valid, xs, w_gu, w_gu, b_gu, b_gu, w_down, b_down)


def _combine_body(dest_ref, y_ref, gt_ref, x_ref, g_ref, b_ref, o_ref, buf, sem, *, tm, alpha):
    def issue(r, _):
        for k in range(TOP_K):
            slot = dest_ref[r * TOP_K + k]
            pltpu.make_async_copy(y_ref.at[pl.ds(slot, 1), :], buf.at[k, pl.ds(r, 1), :], sem).start()
        return 0

    lax.fori_loop(0, tm, issue, 0, unroll=8)
    for k in range(TOP_K):
        pltpu.make_async_copy(y_ref.at[pl.ds(0, tm), :], buf.at[k], sem).wait()

    ffn = gt_ref[:, 0:1] * buf[0]
    for k in range(1, TOP_K):
        ffn = ffn + gt_ref[:, k:k + 1] * buf[k]
    o_ref[...] = _layer_norm(alpha * x_ref[...] + ffn, g_ref[...], b_ref[...])


def _combine(ys, dest_flat, gates, x, g, b, *, alpha, tm=256):
    t, d = x.shape
    tm = _tile(t, tm)
    return pl.pallas_call(
        functools.partial(_combine_body, tm=tm, alpha=alpha),
        grid=(t // tm,),
        in_specs=[pl.BlockSpec((tm * TOP_K,), lambda i: (i,), memory_space=pltpu.SMEM),
                  pl.BlockSpec(memory_space=pl.ANY),
                  pl.BlockSpec((tm, TOP_K), lambda i: (i, 0)),
                  pl.BlockSpec((tm, d), lambda i: (i, 0)),
                  pl.BlockSpec((1, d), lambda i: (0, 0)),
                  pl.BlockSpec((1, d), lambda i: (0, 0))],
        out_specs=pl.BlockSpec((tm, d), lambda i: (i, 0)),
        out_shape=jax.ShapeDtypeStruct((t, d), F32),
        scratch_shapes=[pltpu.VMEM((TOP_K, tm, d), F32), pltpu.SemaphoreType.DMA],
        compiler_params=_cparams(1),
        name="moe_combine",
    )(dest_flat, ys, gates, x, g, b)


def _moe(x, rw, rb, w_gu, b_gu, w_down, b_down, g, b, *, alpha):
    t, d = x.shape
    n_exp = rw.shape[1]
    top_e, gates, rank, counts = _router(x, rw, rb[None, :])

    counts = counts[0].astype(I32)
    padded = (counts + MOE_TM - 1) // MOE_TM * MOE_TM
    pad_end = jnp.cumsum(padded)
    pad_start = pad_end - padded
    dest_flat = (pad_start[top_e] + rank).reshape(-1)
    nb = -(-(t * TOP_K + n_exp * (MOE_TM - 1)) // MOE_TM)
    block_e = jnp.minimum(jnp.searchsorted(pad_end, jnp.arange(nb, dtype=I32) * MOE_TM, side="right"),
                          n_exp - 1).astype(I32)
    n_valid = (pad_end[-1:] // MOE_TM).astype(I32)

    xs = _dispatch(x, dest_flat, pad_end.astype(I32), padded, nb * MOE_TM)
    ys = _experts(xs, block_e, n_valid, w_gu.astype(BF16), b_gu[:, None, :], w_down.astype(BF16), b_down[:, None, :])
    return _combine(ys, dest_flat, gates, x, g, b, alpha=alpha)


def _forget_body(x_ref, wf_ref, bf_ref, c_ref, carry, *, tm):
    @pl.when(pl.program_id(1) == 0)
    def _():
        carry[...] = jnp.zeros_like(carry)

    f = lax.dot_general(wf_ref[...], x_ref[...], (((1,), (1,)), ((), ())),
                        precision=HIGHEST, preferred_element_type=F32) + bf_ref[...]
    log_f = jnp.minimum(f, 0.0) - jnp.log1p(jnp.exp(-jnp.abs(f)))
    row = lax.broadcasted_iota(I32, (tm, tm), 0)
    col = lax.broadcasted_iota(I32, (tm, tm), 1)
    upto = (row <= col).astype(F32)
    c = jnp.dot(log_f, upto, precision=HIGHEST, preferred_element_type=F32) + carry[...]
    c_ref[...] = c
    carry[...] = c[:, tm - 1:tm]


def _forget_cumsum(x, wf_t, bf, *, batch, seq, tm=512):
    d = x.shape[1]
    nh = wf_t.shape[0]
    tm = _tile(seq, tm)
    nt = seq // tm
    return pl.pallas_call(
        functools.partial(_forget_body, tm=tm),
        grid=(batch, nt),
        in_specs=[pl.BlockSpec((tm, d), lambda bi, i: (bi * nt + i, 0)),
                  pl.BlockSpec((nh, d), lambda bi, i: (0, 0)),
                  pl.BlockSpec((nh, 1), lambda bi, i: (0, 0))],
        out_specs=pl.BlockSpec((None, nh, tm), lambda bi, i: (bi, 0, i)),
        out_shape=jax.ShapeDtypeStruct((batch, nh, seq), F32),
        scratch_shapes=[pltpu.VMEM((nh, 1), F32)],
        compiler_params=_cparams(2),
        name="forget_cumsum",
    )(x, wf_t, bf)


def _flash_body(q_ref, k_ref, v_ref, c_ref, o_ref, *, tq, hd):
    i = pl.program_id(2)
    q = q_ref[...]
    q0 = pl.multiple_of(i * tq, tq)
    c_q0 = c_ref[:, pl.ds(q0, tq)][:, 0:1]

    def step(j, carry, masked):
        m, l, acc = carry
        k0 = pl.multiple_of(j * tq, tq)
        k = k_ref[pl.ds(k0, tq), :]
        v = v_ref[pl.ds(k0, tq), :]
        s = lax.dot_general(q, k, (((1,), (1,)), ((), ())), preferred_element_type=F32)
        s = s + (c_q0 - c_ref[:, pl.ds(k0, tq)])
        if masked:
            row = lax.broadcasted_iota(I32, (tq, tq), 0)
            col = lax.broadcasted_iota(I32, (tq, tq), 1)
            s = jnp.where(col <= row, s, -jnp.inf)
        m_new = jnp.maximum(m, jnp.max(s, axis=-1, keepdims=True))
        scale = jnp.exp(m - m_new)
        p = jnp.exp(s - m_new)
        l = scale * l + jnp.sum(p, axis=-1, keepdims=True)
        acc = scale * acc + jnp.dot(p.astype(BF16), v, preferred_element_type=F32)
        return m_new, l, acc

    init = (jnp.full((tq, 1), -jnp.inf, F32), jnp.zeros((tq, 1), F32), jnp.zeros((tq, hd), F32))
    carry = lax.fori_loop(0, i, lambda j, c: step(j, c, False), init)
    _, l, acc = step(i, carry, True)
    o_ref[...] = (acc / l).astype(o_ref.dtype)


def _flash(qkv, c, *, batch, seq, tq=512):
    t = qkv.shape[0]
    d = qkv.shape[1] // 3
    nh = c.shape[1]
    hd = d // nh
    tq = _tile(seq, tq)
    nq = seq // tq
    return pl.pallas_call(
        functools.partial(_flash_body, tq=tq, hd=hd),
        grid=(batch, nh, nq),
        in_specs=[pl.BlockSpec((tq, hd), lambda bi, h, i: (bi * nq + i, h)),
                  pl.BlockSpec((seq, hd), lambda bi, h, i: (bi, nh + h)),
                  pl.BlockSpec((seq, hd), lambda bi, h, i: (bi, 2 * nh + h)),
                  pl.BlockSpec((None, None, 1, seq), lambda bi, h, i: (bi, h, 0, 0))],
        out_specs=pl.BlockSpec((tq, hd), lambda bi, h, i: (bi * nq + i, h)),
        out_shape=jax.ShapeDtypeStruct((t, d), BF16),
        compiler_params=_cparams(3),
        name="fox_attention",
    )(qkv, qkv, qkv, c)


def kernel(x, ab_w_in, ab_pool_w, ab_pool_scale, ab_conv_w, ab_w_out, fox_w_in, fox_b_f, fox_w_o,
           ln1_g, ln1_b, ln2_g, ln2_b, router_w, router_b, w_gu, b_gu, w_down, b_down):
    batch, seq, d = x.shape
    depth = ln1_g.shape[0]
    alpha = (2.0 * depth) ** 0.25
    xt = x.reshape(batch * seq, d)
    for layer in range(depth):
        j = layer // 2
        g1, b1 = ln1_g[layer][None, :], ln1_b[layer][None, :]
        if layer % 2 == 0:
            w_in = ab_w_in[j].astype(BF16)
            h = _matmul(xt, w_in, jnp.ones((1, w_in.shape[1]), F32), F32)
            xt = _mixer0(h, xt, ab_pool_w[j].astype(BF16), ab_pool_scale[j][None, :], ab_conv_w[j],
                         ab_w_out[j].astype(BF16), g1, b1, seq=seq, alpha=alpha)
        else:
            hd = d // fox_b_f.shape[1]
            w_in = fox_w_in[j]
            q_scale = jnp.concatenate([jnp.full((1, d), hd ** -0.5, F32), jnp.ones((1, 2 * d), F32)], axis=1)
            qkv = _matmul(xt, w_in[:, :3 * d].astype(BF16), q_scale, BF16)
            c = _forget_cumsum(xt, w_in[:, 3 * d:].T, fox_b_f[j][:, None], batch=batch, seq=seq)
            o = _flash(qkv, c[:, :, None, :], batch=batch, seq=seq)
            xt = _proj_ln(o, fox_w_o[j].astype(BF16), xt, g1, b1, alpha=alpha)
        xt = _moe(xt, router_w[layer], router_b[layer], w_gu[layer], b_gu[layer], w_down[layer], b_down[layer],
                  ln2_g[layer][None, :], ln2_b[layer][None, :], alpha=alpha)
    return xt.reshape(batch, seq, d)
```

```python
import functools

import jax
import jax.numpy as jnp
from jax import lax
from jax.experimental import pallas as pl
from jax.experimental.pallas import tpu as pltpu

F32 = jnp.float32
BF16 = jnp.bfloat16
I32 = jnp.int32
HIGHEST = lax.Precision.HIGHEST

LN_EPS = 1e-5
POOL_WINDOWS = (2, 4, 8, 16)
CONV_K = 3
HALO = 16
TOP_K = 4
SWIGLU_ALPHA = 1.702
SWIGLU_LIMIT = 7.0
MOE_TM = 512
VMEM_LIMIT = 56 * 2**20


def _cparams(n_axes):
    return pltpu.CompilerParams(dimension_semantics=("arbitrary",) * n_axes, vmem_limit_bytes=VMEM_LIMIT)


def _tile(n, pref):
    t = min(n, pref)
    while n % t:
        t -= 128
    assert t > 0, (n, pref)
    return t


def _layer_norm(z, g, b):
    mu = jnp.mean(z, axis=-1, keepdims=True)
    zc = z - mu
    var = jnp.mean(zc * zc, axis=-1, keepdims=True)
    return zc * lax.rsqrt(var + LN_EPS) * g + b


def _mm_body(a_ref, b_ref, s_ref, o_ref, a_bf):
    @pl.when(pl.program_id(1) == 0)
    def _():
        a_bf[...] = a_ref[...].astype(BF16)

    acc = jnp.dot(a_bf[...], b_ref[...], preferred_element_type=F32)
    o_ref[...] = (acc * s_ref[...]).astype(o_ref.dtype)


def _matmul(a, b, col_scale, out_dtype, tm=1024, tn=1024):
    m, k = a.shape
    n = b.shape[1]
    tm, tn = _tile(m, tm), _tile(n, tn)
    return pl.pallas_call(
        _mm_body,
        grid=(m // tm, n // tn),
        in_specs=[pl.BlockSpec((tm, k), lambda i, j: (i, 0)),
                  pl.BlockSpec((k, tn), lambda i, j: (0, j)),
                  pl.BlockSpec((1, tn), lambda i, j: (0, j))],
        out_specs=pl.BlockSpec((tm, tn), lambda i, j: (i, j)),
        out_shape=jax.ShapeDtypeStruct((m, n), out_dtype),
        scratch_shapes=[pltpu.VMEM((tm, k), BF16)],
        compiler_params=_cparams(2),
        name="dense_matmul",
    )(a, b, col_scale)


def _mixer0_body(h_ref, halo_ref, x_ref, pw_ref, ps_ref, cw_ref, wo_ref, g_ref, b_ref, o_ref,
                 *, tm, tiles_per_seq, pool_w, conv_w, alpha):
    t_in_seq = pl.program_id(0) % tiles_per_seq
    first = t_in_seq == 0
    group = pool_w // len(POOL_WINDOWS)
    c0, c1, c2 = pool_w, pool_w + conv_w, pool_w + 2 * conv_w

    a = h_ref[:, 0:pool_w]
    a_halo = jnp.where(first, 0.0, halo_ref[:, 0:pool_w])
    s = jnp.concatenate([a_halo, a], axis=0)
    pos = t_in_seq * tm + lax.broadcasted_iota(I32, (tm, 1), 0)
    pooled = []
    for gi, w in enumerate(POOL_WINDOWS):
        s = s + pltpu.roll(s, w // 2, 0)
        inv_count = 1.0 / jnp.minimum(pos + 1, w).astype(F32)
        p = s[HALO:, 0:group] * inv_count - a[:, gi * group:(gi + 1) * group]
        pooled.append(jnp.dot(p.astype(BF16), pw_ref[gi], preferred_element_type=F32))
        if gi + 1 < len(POOL_WINDOWS):
            s = s[:, group:]
    y_pool = jnp.concatenate(pooled, axis=1) * ps_ref[...]

    z = h_ref[:, c1:c2] * h_ref[:, c2:]
    z_halo = jnp.where(first, 0.0, halo_ref[:, c1:c2] * halo_ref[:, c2:])
    z_ext = jnp.concatenate([z_halo, z], axis=0)
    conv = (cw_ref[2:3, :] * z_ext + cw_ref[1:2, :] * pltpu.roll(z_ext, 1, 0)
            + cw_ref[0:1, :] * pltpu.roll(z_ext, 2, 0))
    y_conv = h_ref[:, c0:c1] * conv[HALO:]

    y = jnp.concatenate([y_pool, y_conv], axis=1).astype(BF16)
    mix = jnp.dot(y, wo_ref[...], preferred_element_type=F32)
    o_ref[...] = _layer_norm(alpha * x_ref[...] + mix, g_ref[...], b_ref[...])


def _mixer0(h, x, pool_w, pool_scale, conv_w, w_out, g, b, *, seq, alpha, tm=256):
    t, d = x.shape
    width = h.shape[1]
    pw = pool_scale.shape[1]
    cw = conv_w.shape[1]
    tm = _tile(seq, tm)
    body = functools.partial(_mixer0_body, tm=tm, tiles_per_seq=seq // tm, pool_w=pw, conv_w=cw, alpha=alpha)
    full = lambda shape: pl.BlockSpec(shape, lambda i: (0,) * len(shape))
    return pl.pallas_call(
        body,
        grid=(t // tm,),
        in_specs=[pl.BlockSpec((tm, width), lambda i: (i, 0)),
                  pl.BlockSpec((HALO, width), lambda i: (jnp.maximum(i * (tm // HALO) - 1, 0), 0)),
                  pl.BlockSpec((tm, d), lambda i: (i, 0)),
                  full(pool_w.shape), full(pool_scale.shape), full(conv_w.shape), full(w_out.shape),
                  full(g.shape), full(b.shape)],
        out_specs=pl.BlockSpec((tm, d), lambda i: (i, 0)),
        out_shape=jax.ShapeDtypeStruct((t, d), F32),
        compiler_params=_cparams(1),
        name="pool_conv_mixer",
    )(h, h, x, pool_w, pool_scale, conv_w, w_out, g, b)


def _proj_ln_body(a_ref, w_ref, x_ref, g_ref, b_ref, o_ref, *, alpha):
    mix = jnp.dot(a_ref[...], w_ref[...], preferred_element_type=F32)
    o_ref[...] = _layer_norm(alpha * x_ref[...] + mix, g_ref[...], b_ref[...])


def _proj_ln(a, w, x, g, b, *, alpha, tm=512):
    t, d = x.shape
    tm = _tile(t, tm)
    full = lambda shape: pl.BlockSpec(shape, lambda i: (0,) * len(shape))
    return pl.pallas_call(
        functools.partial(_proj_ln_body, alpha=alpha),
        grid=(t // tm,),
        in_specs=[pl.BlockSpec((tm, a.shape[1]), lambda i: (i, 0)), full(w.shape),
                  pl.BlockSpec((tm, d), lambda i: (i, 0)), full(g.shape), full(b.shape)],
        out_specs=pl.BlockSpec((tm, d), lambda i: (i, 0)),
        out_shape=jax.ShapeDtypeStruct((t, d), F32),
        compiler_params=_cparams(1),
        name="proj_residual_ln",
    )(a, w, x, g, b)


def _select_columns(cols, tm):
    lane = lax.broadcasted_iota(I32, (tm, len(cols)), 1)
    out = jnp.broadcast_to(cols[-1], (tm, len(cols)))
    for k in range(len(cols) - 2, -1, -1):
        out = jnp.where(lane == k, cols[k], out)
    return out


def _router_body(x_ref, rw_ref, rb_ref, te_ref, gt_ref, rk_ref, cnt_ref, carry, *, tm, n_exp):
    @pl.when(pl.program_id(0) == 0)
    def _():
        carry[...] = jnp.zeros_like(carry)

    logits = jnp.dot(x_ref[...], rw_ref[...], precision=HIGHEST, preferred_element_type=F32) + rb_ref[...]
    lane = lax.broadcasted_iota(I32, (tm, n_exp), 1)
    vals, idxs, hots = [], [], []
    rest = logits
    for _ in range(TOP_K):
        m = jnp.max(rest, axis=-1, keepdims=True)
        idx = jnp.min(jnp.where(rest == m, lane, n_exp), axis=-1, keepdims=True)
        hot = lane == idx
        rest = jnp.where(hot, -jnp.inf, rest)
        vals.append(m)
        idxs.append(idx)
        hots.append(hot)
    exps = [jnp.exp(v - vals[0]) for v in vals]
    inv_sum = 1.0 / sum(exps)
    gates = [e * inv_sum for e in exps]

    cnt = sum(h.astype(F32) for h in hots)
    row = lax.broadcasted_iota(I32, (tm, tm), 0)
    col = lax.broadcasted_iota(I32, (tm, tm), 1)
    earlier = (col < row).astype(BF16)
    before = jnp.dot(earlier, cnt.astype(BF16), preferred_element_type=F32) + carry[...]
    ranks = [jnp.sum(jnp.where(h, before, 0.0), axis=-1, keepdims=True).astype(I32) for h in hots]

    te_ref[...] = _select_columns(idxs, tm)
    gt_ref[...] = _select_columns(gates, tm)
    rk_ref[...] = _select_columns(ranks, tm)
    carry[...] = carry[...] + jnp.sum(cnt, axis=0, keepdims=True)
    cnt_ref[...] = carry[...]


def _router(x, rw, rb, tm=512):
    t, d = x.shape
    n_exp = rw.shape[1]
    tm = _tile(t, tm)
    small = lambda: pl.BlockSpec((tm, TOP_K), lambda i: (i, 0))
    return pl.pallas_call(
        functools.partial(_router_body, tm=tm, n_exp=n_exp),
        grid=(t // tm,),
        in_specs=[pl.BlockSpec((tm, d), lambda i: (i, 0)),
                  pl.BlockSpec((d, n_exp), lambda i: (0, 0)),
                  pl.BlockSpec((1, n_exp), lambda i: (0, 0))],
        out_specs=[small(), small(), small(), pl.BlockSpec((1, n_exp), lambda i: (0, 0))],
        out_shape=[jax.ShapeDtypeStruct((t, TOP_K), I32), jax.ShapeDtypeStruct((t, TOP_K), F32),
                   jax.ShapeDtypeStruct((t, TOP_K), I32), jax.ShapeDtypeStruct((1, n_exp), F32)],
        scratch_shapes=[pltpu.VMEM((1, n_exp), F32)],
        compiler_params=_cparams(1),
        name="moe_router",
    )(x, rw, rb)


def _dispatch_body(pe_ref, pd_ref, dest_ref, x_ref, xs_ref, zbuf, sem, zsem, *, tm, n_exp, n_blocks):
    def zero_block(start):
        return pltpu.make_async_copy(zbuf, xs_ref.at[pl.ds(pl.multiple_of(start, MOE_TM), MOE_TM), :], zsem)

    @pl.when(pl.program_id(0) == 0)
    def _():
        zbuf[...] = jnp.zeros_like(zbuf)
        n_used = pe_ref[n_exp - 1] // MOE_TM
        for e in range(n_exp):
            @pl.when(pd_ref[e] > 0)
            def _():
                zero_block(pe_ref[e] - MOE_TM).start()
        lax.fori_loop(n_used, n_blocks, lambda bi, c: (zero_block(bi * MOE_TM).start(), c)[1], 0)
        for e in range(n_exp):
            @pl.when(pd_ref[e] > 0)
            def _():
                zero_block(pe_ref[e] - MOE_TM).wait()
        lax.fori_loop(n_used, n_blocks, lambda bi, c: (zero_block(bi * MOE_TM).wait(), c)[1], 0)

    def issue(r, _):
        for k in range(TOP_K):
            slot = dest_ref[r * TOP_K + k]
            pltpu.make_async_copy(x_ref.at[pl.ds(r, 1), :], xs_ref.at[pl.ds(slot, 1), :], sem).start()
        return 0

    lax.fori_loop(0, tm, issue, 0, unroll=8)
    for _ in range(TOP_K):
        pltpu.make_async_copy(x_ref, xs_ref.at[pl.ds(0, tm), :], sem).wait()


def _dispatch(x, dest_flat, pad_end, padded, n_slots, tm=256):
    t, d = x.shape
    tm = _tile(t, tm)
    n_exp = pad_end.shape[0]
    grid_spec = pltpu.PrefetchScalarGridSpec(
        num_scalar_prefetch=2,
        grid=(t // tm,),
        in_specs=[pl.BlockSpec((tm * TOP_K,), lambda i, pe, pd: (i,), memory_space=pltpu.SMEM),
                  pl.BlockSpec((tm, d), lambda i, pe, pd: (i, 0))],
        out_specs=pl.BlockSpec(memory_space=pl.ANY),
        scratch_shapes=[pltpu.VMEM((MOE_TM, d), F32), pltpu.SemaphoreType.DMA, pltpu.SemaphoreType.DMA],
    )
    return pl.pallas_call(
        functools.partial(_dispatch_body, tm=tm, n_exp=n_exp, n_blocks=n_slots // MOE_TM),
        grid_spec=grid_spec,
        out_shape=jax.ShapeDtypeStruct((n_slots, d), F32),
        compiler_params=_cparams(1),
        name="moe_dispatch",
    )(pad_end, padded, dest_flat, x)


def _experts_body(be_ref, nv_ref, x_ref, wg_ref, wl_ref, bg_ref, bl_ref, wd_ref, bd_ref, o_ref, xb):
    i, j = pl.program_id(0), pl.program_id(1)

    @pl.when(i < nv_ref[0])
    def _():
        @pl.when(j == 0)
        def _():
            xb[...] = x_ref[...].astype(BF16)

        x = xb[...]
        glu = jnp.dot(x, wg_ref[...], preferred_element_type=F32) + bg_ref[...]
        lin = jnp.dot(x, wl_ref[...], preferred_element_type=F32) + bl_ref[...]
        glu = jnp.minimum(glu, SWIGLU_LIMIT)
        lin = jnp.clip(lin, -SWIGLU_LIMIT, SWIGLU_LIMIT)
        hid = (lin + 1.0) * (glu * jax.nn.sigmoid(SWIGLU_ALPHA * glu))
        part = jnp.dot(hid.astype(BF16), wd_ref[...], preferred_element_type=F32)

        @pl.when(j == 0)
        def _():
            o_ref[...] = part + bd_ref[...]

        @pl.when(j > 0)
        def _():
            o_ref[...] += part

    @pl.when(jnp.logical_and(i >= nv_ref[0], j == 0))
    def _():
        o_ref[...] = jnp.zeros_like(o_ref)


def _experts(xs, block_e, n_valid, w_gu, b_gu, w_down, b_down, layer, tf=1024):
    n_slots, d = xs.shape
    f = w_gu.shape[3] // 2
    tf = _tile(f, tf)
    nf = f // tf
    nb = n_slots // MOE_TM

    def exp(i, be, nv):
        return be[jnp.minimum(i, nv[0] - 1)]

    def fch(i, j, nv):
        return jnp.where(i < nv[0], j, nf - 1)

    grid_spec = pltpu.PrefetchScalarGridSpec(
        num_scalar_prefetch=2,
        grid=(nb, nf),
        in_specs=[pl.BlockSpec((MOE_TM, d), lambda i, j, be, nv: (jnp.minimum(i, nv[0] - 1), 0)),
                  pl.BlockSpec((None, None, d, tf), lambda i, j, be, nv: (layer, exp(i, be, nv), 0, fch(i, j, nv))),
                  pl.BlockSpec((None, None, d, tf), lambda i, j, be, nv: (layer, exp(i, be, nv), 0, nf + fch(i, j, nv))),
                  pl.BlockSpec((None, None, 1, tf), lambda i, j, be, nv: (layer, exp(i, be, nv), 0, fch(i, j, nv))),
                  pl.BlockSpec((None, None, 1, tf), lambda i, j, be, nv: (layer, exp(i, be, nv), 0, nf + fch(i, j, nv))),
                  pl.BlockSpec((None, None, tf, d), lambda i, j, be, nv: (layer, exp(i, be, nv), fch(i, j, nv), 0)),
                  pl.BlockSpec((None, None, 1, d), lambda i, j, be, nv: (layer, exp(i, be, nv), 0, 0))],
        out_specs=pl.BlockSpec((MOE_TM, d), lambda i, j, be, nv: (i, 0)),
        scratch_shapes=[pltpu.VMEM((MOE_TM, d), BF16)],
    )
    return pl.pallas_call(
        _experts_body,
        grid_spec=grid_spec,
        out_shape=jax.ShapeDtypeStruct((n_slots, d), F32),
        compiler_params=_cparams(2),
        name="moe_experts",
    )(block_e, n_valid, xs, w_gu, w_gu, b_gu, b_gu, w_down, b_down)


def _combine_body(dest_ref, y_ref, gt_ref, x_ref, g_ref, b_ref, o_ref, buf, sem, *, tm, alpha):
    def issue(r, _):
        for k in range(TOP_K):
            slot = dest_ref[r * TOP_K + k]
            pltpu.make_async_copy(y_ref.at[pl.ds(slot, 1), :], buf.at[k, pl.ds(r, 1), :], sem).start()
        return 0

    lax.fori_loop(0, tm, issue, 0, unroll=8)
    for k in range(TOP_K):
        pltpu.make_async_copy(y_ref.at[pl.ds(0, tm), :], buf.at[k], sem).wait()

    ffn = gt_ref[:, 0:1] * buf[0]
    for k in range(1, TOP_K):
        ffn = ffn + gt_ref[:, k:k + 1] * buf[k]
    o_ref[...] = _layer_norm(alpha * x_ref[...] + ffn, g_ref[...], b_ref[...])


def _combine(ys, dest_flat, gates, x, g, b, *, alpha, tm=256):
    t, d = x.shape
    tm = _tile(t, tm)
    return pl.pallas_call(
        functools.partial(_combine_body, tm=tm, alpha=alpha),
        grid=(t // tm,),
        in_specs=[pl.BlockSpec((tm * TOP_K,), lambda i: (i,), memory_space=pltpu.SMEM),
                  pl.BlockSpec(memory_space=pl.ANY),
                  pl.BlockSpec((tm, TOP_K), lambda i: (i, 0)),
                  pl.BlockSpec((tm, d), lambda i: (i, 0)),
                  pl.BlockSpec((1, d), lambda i: (0, 0)),
                  pl.BlockSpec((1, d), lambda i: (0, 0))],
        out_specs=pl.BlockSpec((tm, d), lambda i: (i, 0)),
        out_shape=jax.ShapeDtypeStruct((t, d), F32),
        scratch_shapes=[pltpu.VMEM((TOP_K, tm, d), F32), pltpu.SemaphoreType.DMA],
        compiler_params=_cparams(1),
        name="moe_combine",
    )(dest_flat, ys, gates, x, g, b)


def _moe(x, rw, rb, w_gu, b_gu, w_down, b_down, layer, g, b, *, alpha):
    t, d = x.shape
    n_exp = rw.shape[1]
    top_e, gates, rank, counts = _router(x, rw, rb[None, :])

    counts = counts[0].astype(I32)
    padded = (counts + MOE_TM - 1) // MOE_TM * MOE_TM
    pad_end = jnp.cumsum(padded)
    pad_start = pad_end - padded
    dest_flat = (pad_start[top_e] + rank).reshape(-1)
    nb = -(-(t * TOP_K + n_exp * (MOE_TM - 1)) // MOE_TM)
    block_start = jnp.arange(nb, dtype=I32) * MOE_TM
    block_e = jnp.minimum(jnp.sum((pad_end[None, :] <= block_start[:, None]).astype(I32), axis=1), n_exp - 1)
    n_valid = pad_end[-1:] // MOE_TM

    xs = _dispatch(x, dest_flat, pad_end, padded, nb * MOE_TM)
    ys = _experts(xs, block_e, n_valid, w_gu, b_gu, w_down, b_down, layer)
    return _combine(ys, dest_flat, gates, x, g, b, alpha=alpha)


def _forget_body(x_ref, wf_ref, bf_ref, c_ref, carry, *, tm):
    @pl.when(pl.program_id(1) == 0)
    def _():
        carry[...] = jnp.zeros_like(carry)

    f = jnp.dot(x_ref[...], wf_ref[...], precision=HIGHEST, preferred_element_type=F32) + bf_ref[...]
    log_f = jnp.minimum(f, 0.0) - jnp.log1p(jnp.exp(-jnp.abs(f)))
    row = lax.broadcasted_iota(I32, (tm, tm), 0)
    col = lax.broadcasted_iota(I32, (tm, tm), 1)
    upto = (col <= row).astype(F32)
    c = jnp.dot(upto, log_f, precision=HIGHEST, preferred_element_type=F32) + carry[...]
    c_ref[...] = c
    carry[...] = c[tm - 1:tm, :]


def _forget_cumsum(x, wf, bf, *, batch, seq, tm=512):
    t, d = x.shape
    nh = wf.shape[1]
    tm = _tile(seq, tm)
    nt = seq // tm
    return pl.pallas_call(
        functools.partial(_forget_body, tm=tm),
        grid=(batch, nt),
        in_specs=[pl.BlockSpec((tm, d), lambda bi, i: (bi * nt + i, 0)),
                  pl.BlockSpec((d, nh), lambda bi, i: (0, 0)),
                  pl.BlockSpec((1, nh), lambda bi, i: (0, 0))],
        out_specs=pl.BlockSpec((tm, nh), lambda bi, i: (bi * nt + i, 0)),
        out_shape=jax.ShapeDtypeStruct((t, nh), F32),
        scratch_shapes=[pltpu.VMEM((1, nh), F32)],
        compiler_params=_cparams(2),
        name="forget_cumsum",
    )(x, wf, bf)


LOG2E = 1.4426950408889634
N_BIAS_PARTS = 3


def _kv_prep_body(k_ref, v_ref, c_ref, ka_ref, va_ref, *, tm, hd, nh):
    h = pl.program_id(1)
    head = lax.broadcasted_iota(I32, (tm, nh), 1)
    c = jnp.sum(jnp.where(head == h, c_ref[...], 0.0), axis=-1, keepdims=True)
    bias = -LOG2E * c
    lane = lax.broadcasted_iota(I32, (tm, hd), 1)
    extra = jnp.zeros((tm, hd), F32)
    rest = bias
    for part in range(N_BIAS_PARTS):
        piece = rest.astype(BF16).astype(F32)
        extra = jnp.where(lane == part, piece, extra)
        rest = rest - piece
    ka_ref[...] = jnp.concatenate([k_ref[...], extra.astype(BF16)], axis=1)
    ones_col = jnp.where(lane == 0, 1.0, 0.0).astype(BF16)
    va_ref[...] = jnp.concatenate([v_ref[...], ones_col], axis=1)


def _kv_prep(qkv, c, *, batch, seq, nh, tm=1024):
    d = qkv.shape[1] // 3
    hd = d // nh
    tm = _tile(seq, tm)
    nt = seq // tm
    out = jax.ShapeDtypeStruct((batch, nh, seq, 2 * hd), BF16)
    out_spec = pl.BlockSpec((None, None, tm, 2 * hd), lambda bi, h, i: (bi, h, i, 0))
    return pl.pallas_call(
        functools.partial(_kv_prep_body, tm=tm, hd=hd, nh=nh),
        grid=(batch, nh, nt),
        in_specs=[pl.BlockSpec((tm, hd), lambda bi, h, i: (bi * nt + i, nh + h)),
                  pl.BlockSpec((tm, hd), lambda bi, h, i: (bi * nt + i, 2 * nh + h)),
                  pl.BlockSpec((tm, nh), lambda bi, h, i: (bi * nt + i, 0))],
        out_specs=[out_spec, out_spec],
        out_shape=[out, out],
        compiler_params=_cparams(3),
        name="fox_kv_prep",
    )(qkv, qkv, c)


def _flash_body(q_ref, ka_ref, va_ref, o_ref, *, tq, tk, hd):
    i = pl.program_id(2)
    lane = lax.broadcasted_iota(I32, (tq, hd), 1)
    q = jnp.concatenate([q_ref[...], jnp.where(lane < N_BIAS_PARTS, 1.0, 0.0).astype(BF16)], axis=1)
    ratio = tq // tk

    def scores(j, masked):
        k0 = pl.multiple_of(j * tk, tk)
        s = lax.dot_general(q, ka_ref[pl.ds(k0, tk), :], (((1,), (1,)), ((), ())), preferred_element_type=F32)
        if masked:
            row = i * tq + lax.broadcasted_iota(I32, (tq, tk), 0)
            col = j * tk + lax.broadcasted_iota(I32, (tq, tk), 1)
            s = jnp.where(col <= row, s, -jnp.inf)
        return s

    def update(j, s, carry):
        m, l, acc = carry
        k0 = pl.multiple_of(j * tk, tk)
        m_new = jnp.maximum(m, jnp.max(s, axis=-1, keepdims=True))
        p = jnp.exp2(s - m_new).astype(BF16)
        pv = jnp.dot(p, va_ref[pl.ds(k0, tk), :], preferred_element_type=F32)
        scale = jnp.exp2(m - m_new)
        return m_new, scale * l + pv[:, hd:hd + 1], scale * acc + pv[:, 0:hd]

    def group(j0, carry, masked):
        tiles = [scores(j0 + r, masked) for r in range(ratio)]
        for r in range(ratio):
            carry = update(j0 + r, tiles[r], carry)
        return carry

    init = (jnp.full((tq, 1), -jnp.inf, F32), jnp.zeros((tq, 1), F32), jnp.zeros((tq, hd), F32))
    carry = lax.fori_loop(0, i, lambda g, c: group(g * ratio, c, False), init)
    _, l, acc = group(i * ratio, carry, True)
    o_ref[...] = (acc / l).astype(o_ref.dtype)


def _flash(qkv, ka, va, *, batch, seq, tq=1024, tk=512):
    t = qkv.shape[0]
    d = qkv.shape[1] // 3
    nh = ka.shape[1]
    hd = d // nh
    tq = _tile(seq, tq)
    tk = _tile(tq, tk)
    nq = seq // tq
    kv_spec = pl.BlockSpec((None, None, seq, 2 * hd), lambda bi, h, i: (bi, h, 0, 0))
    return pl.pallas_call(
        functools.partial(_flash_body, tq=tq, tk=tk, hd=hd),
        grid=(batch, nh, nq),
        in_specs=[pl.BlockSpec((tq, hd), lambda bi, h, i: (bi * nq + i, h)), kv_spec, kv_spec],
        out_specs=pl.BlockSpec((tq, hd), lambda bi, h, i: (bi * nq + i, h)),
        out_shape=jax.ShapeDtypeStruct((t, d), BF16),
        compiler_params=_cparams(3),
        name="fox_attention",
    )(qkv, ka, va)


def kernel(x, ab_w_in, ab_pool_w, ab_pool_scale, ab_conv_w, ab_w_out, fox_w_in, fox_b_f, fox_w_o,
           ln1_g, ln1_b, ln2_g, ln2_b, router_w, router_b, w_gu, b_gu, w_down, b_down):
    batch, seq, d = x.shape
    depth = ln1_g.shape[0]
    alpha = (2.0 * depth) ** 0.25
    xt = x.reshape(batch * seq, d)
    w_gu_bf, w_down_bf = w_gu.astype(BF16), w_down.astype(BF16)
    b_gu4, b_down4 = b_gu[:, :, None, :], b_down[:, :, None, :]
    for layer in range(depth):
        j = layer // 2
        g1, b1 = ln1_g[layer][None, :], ln1_b[layer][None, :]
        if layer % 2 == 0:
            w_in = ab_w_in[j].astype(BF16)
            h = _matmul(xt, w_in, jnp.ones((1, w_in.shape[1]), F32), F32)
            xt = _mixer0(h, xt, ab_pool_w[j].astype(BF16), ab_pool_scale[j][None, :], ab_conv_w[j],
                         ab_w_out[j].astype(BF16), g1, b1, seq=seq, alpha=alpha)
        else:
            nh = fox_b_f.shape[1]
            w_in = fox_w_in[j]
            q_scale = jnp.concatenate([jnp.full((1, d), LOG2E * (d // nh) ** -0.5, F32), jnp.ones((1, 2 * d), F32)],
                                      axis=1)
            qkv = _matmul(xt, w_in[:, :3 * d].astype(BF16), q_scale, BF16)
            c = _forget_cumsum(xt, w_in[:, 3 * d:], fox_b_f[j][None, :], batch=batch, seq=seq)
            ka, va = _kv_prep(qkv, c, batch=batch, seq=seq, nh=nh)
            o = _flash(qkv, ka, va, batch=batch, seq=seq)
            xt = _proj_ln(o, fox_w_o[j].astype(BF16), xt, g1, b1, alpha=alpha)
        xt = _moe(xt, router_w[layer], router_b[layer], w_gu_bf, b_gu4, w_down_bf, b_down4, layer,
                  ln2_g[layer][None, :], ln2_b[layer][None, :], alpha=alpha)
    return xt.reshape(batch, seq, d)
```

```python
import functools

import jax
import jax.numpy as jnp
from jax import lax
from jax.experimental import pallas as pl
from jax.experimental.pallas import tpu as pltpu

F32 = jnp.float32
BF16 = jnp.bfloat16
I32 = jnp.int32
HIGHEST = lax.Precision.HIGHEST

LN_EPS = 1e-5
POOL_WINDOWS = (2, 4, 8, 16)
CONV_K = 3
HALO = 16
TOP_K = 4
SWIGLU_ALPHA = 1.702
SWIGLU_LIMIT = 7.0
MOE_TM = 512
VMEM_LIMIT = 56 * 2**20


def _cparams(n_axes):
    return pltpu.CompilerParams(dimension_semantics=("arbitrary",) * n_axes, vmem_limit_bytes=VMEM_LIMIT)


def _tile(n, pref):
    t = min(n, pref)
    while n % t:
        t -= 128
    assert t > 0, (n, pref)
    return t


def _layer_norm(z, g, b):
    mu = jnp.mean(z, axis=-1, keepdims=True)
    zc = z - mu
    var = jnp.mean(zc * zc, axis=-1, keepdims=True)
    return zc * lax.rsqrt(var + LN_EPS) * g + b


def _mm_body(a_ref, b_ref, s_ref, o_ref, a_bf):
    @pl.when(pl.program_id(1) == 0)
    def _():
        a_bf[...] = a_ref[...].astype(BF16)

    acc = jnp.dot(a_bf[...], b_ref[...], preferred_element_type=F32)
    o_ref[...] = (acc * s_ref[...]).astype(o_ref.dtype)


def _matmul(a, b, col_scale, out_dtype, tm=1024, tn=1024):
    m, k = a.shape
    n = b.shape[1]
    tm, tn = _tile(m, tm), _tile(n, tn)
    return pl.pallas_call(
        _mm_body,
        grid=(m // tm, n // tn),
        in_specs=[pl.BlockSpec((tm, k), lambda i, j: (i, 0)),
                  pl.BlockSpec((k, tn), lambda i, j: (0, j)),
                  pl.BlockSpec((1, tn), lambda i, j: (0, j))],
        out_specs=pl.BlockSpec((tm, tn), lambda i, j: (i, j)),
        out_shape=jax.ShapeDtypeStruct((m, n), out_dtype),
        scratch_shapes=[pltpu.VMEM((tm, k), BF16)],
        compiler_params=_cparams(2),
        name="dense_matmul",
    )(a, b, col_scale)


def _mixer0_body(h_ref, halo_ref, x_ref, pw_ref, ps_ref, cw_ref, wo_ref, g_ref, b_ref, o_ref,
                 *, tm, tiles_per_seq, pool_w, conv_w, alpha):
    t_in_seq = pl.program_id(0) % tiles_per_seq
    first = t_in_seq == 0
    group = pool_w // len(POOL_WINDOWS)
    c0, c1, c2 = pool_w, pool_w + conv_w, pool_w + 2 * conv_w

    a = h_ref[:, 0:pool_w]
    a_halo = jnp.where(first, 0.0, halo_ref[:, 0:pool_w])
    s = jnp.concatenate([a_halo, a], axis=0)
    pos = t_in_seq * tm + lax.broadcasted_iota(I32, (tm, 1), 0)
    pooled = []
    for gi, w in enumerate(POOL_WINDOWS):
        s = s + pltpu.roll(s, w // 2, 0)
        inv_count = 1.0 / jnp.minimum(pos + 1, w).astype(F32)
        p = s[HALO:, 0:group] * inv_count - a[:, gi * group:(gi + 1) * group]
        pooled.append(jnp.dot(p.astype(BF16), pw_ref[gi], preferred_element_type=F32))
        if gi + 1 < len(POOL_WINDOWS):
            s = s[:, group:]
    y_pool = jnp.concatenate(pooled, axis=1) * ps_ref[...]

    z = h_ref[:, c1:c2] * h_ref[:, c2:]
    z_halo = jnp.where(first, 0.0, halo_ref[:, c1:c2] * halo_ref[:, c2:])
    z_ext = jnp.concatenate([z_halo, z], axis=0)
    conv = (cw_ref[2:3, :] * z_ext + cw_ref[1:2, :] * pltpu.roll(z_ext, 1, 0)
            + cw_ref[0:1, :] * pltpu.roll(z_ext, 2, 0))
    y_conv = h_ref[:, c0:c1] * conv[HALO:]

    y = jnp.concatenate([y_pool, y_conv], axis=1).astype(BF16)
    mix = jnp.dot(y, wo_ref[...], preferred_element_type=F32)
    o_ref[...] = _layer_norm(alpha * x_ref[...] + mix, g_ref[...], b_ref[...])


def _mixer0(h, x, pool_w, pool_scale, conv_w, w_out, g, b, *, seq, alpha, tm=256):
    t, d = x.shape
    width = h.shape[1]
    pw = pool_scale.shape[1]
    cw = conv_w.shape[1]
    tm = _tile(seq, tm)
    body = functools.partial(_mixer0_body, tm=tm, tiles_per_seq=seq // tm, pool_w=pw, conv_w=cw, alpha=alpha)
    full = lambda shape: pl.BlockSpec(shape, lambda i: (0,) * len(shape))
    return pl.pallas_call(
        body,
        grid=(t // tm,),
        in_specs=[pl.BlockSpec((tm, width), lambda i: (i, 0)),
                  pl.BlockSpec((HALO, width), lambda i: (jnp.maximum(i * (tm // HALO) - 1, 0), 0)),
                  pl.BlockSpec((tm, d), lambda i: (i, 0)),
                  full(pool_w.shape), full(pool_scale.shape), full(conv_w.shape), full(w_out.shape),
                  full(g.shape), full(b.shape)],
        out_specs=pl.BlockSpec((tm, d), lambda i: (i, 0)),
        out_shape=jax.ShapeDtypeStruct((t, d), F32),
        compiler_params=_cparams(1),
        name="pool_conv_mixer",
    )(h, h, x, pool_w, pool_scale, conv_w, w_out, g, b)


def _proj_ln_body(a_ref, w_ref, x_ref, g_ref, b_ref, o_ref, *, alpha):
    mix = jnp.dot(a_ref[...], w_ref[...], preferred_element_type=F32)
    o_ref[...] = _layer_norm(alpha * x_ref[...] + mix, g_ref[...], b_ref[...])


def _proj_ln(a, w, x, g, b, *, alpha, tm=512):
    t, d = x.shape
    tm = _tile(t, tm)
    full = lambda shape: pl.BlockSpec(shape, lambda i: (0,) * len(shape))
    return pl.pallas_call(
        functools.partial(_proj_ln_body, alpha=alpha),
        grid=(t // tm,),
        in_specs=[pl.BlockSpec((tm, a.shape[1]), lambda i: (i, 0)), full(w.shape),
                  pl.BlockSpec((tm, d), lambda i: (i, 0)), full(g.shape), full(b.shape)],
        out_specs=pl.BlockSpec((tm, d), lambda i: (i, 0)),
        out_shape=jax.ShapeDtypeStruct((t, d), F32),
        compiler_params=_cparams(1),
        name="proj_residual_ln",
    )(a, w, x, g, b)


def _select_columns(cols, tm):
    lane = lax.broadcasted_iota(I32, (tm, len(cols)), 1)
    out = jnp.broadcast_to(cols[-1], (tm, len(cols)))
    for k in range(len(cols) - 2, -1, -1):
        out = jnp.where(lane == k, cols[k], out)
    return out


def _router_body(x_ref, rw_ref, rb_ref, te_ref, gt_ref, rk_ref, cnt_ref, carry, *, tm, n_exp):
    @pl.when(pl.program_id(0) == 0)
    def _():
        carry[...] = jnp.zeros_like(carry)

    logits = jnp.dot(x_ref[...], rw_ref[...], precision=HIGHEST, preferred_element_type=F32) + rb_ref[...]
    lane = lax.broadcasted_iota(I32, (tm, n_exp), 1)
    vals, idxs, hots = [], [], []
    rest = logits
    for _ in range(TOP_K):
        m = jnp.max(rest, axis=-1, keepdims=True)
        idx = jnp.min(jnp.where(rest == m, lane, n_exp), axis=-1, keepdims=True)
        hot = lane == idx
        rest = jnp.where(hot, -jnp.inf, rest)
        vals.append(m)
        idxs.append(idx)
        hots.append(hot)
    exps = [jnp.exp(v - vals[0]) for v in vals]
    inv_sum = 1.0 / sum(exps)
    gates = [e * inv_sum for e in exps]

    cnt = sum(h.astype(F32) for h in hots)
    row = lax.broadcasted_iota(I32, (tm, tm), 0)
    col = lax.broadcasted_iota(I32, (tm, tm), 1)
    earlier = (col < row).astype(BF16)
    before = jnp.dot(earlier, cnt.astype(BF16), preferred_element_type=F32) + carry[...]
    ranks = [jnp.sum(jnp.where(h, before, 0.0), axis=-1, keepdims=True).astype(I32) for h in hots]

    te_ref[...] = _select_columns(idxs, tm)
    gt_ref[...] = _select_columns(gates, tm)
    rk_ref[...] = _select_columns(ranks, tm)
    carry[...] = carry[...] + jnp.sum(cnt, axis=0, keepdims=True)
    cnt_ref[...] = carry[...]


def _router(x, rw, rb, tm=512):
    t, d = x.shape
    n_exp = rw.shape[1]
    tm = _tile(t, tm)
    small = lambda: pl.BlockSpec((tm, TOP_K), lambda i: (i, 0))
    return pl.pallas_call(
        functools.partial(_router_body, tm=tm, n_exp=n_exp),
        grid=(t // tm,),
        in_specs=[pl.BlockSpec((tm, d), lambda i: (i, 0)),
                  pl.BlockSpec((d, n_exp), lambda i: (0, 0)),
                  pl.BlockSpec((1, n_exp), lambda i: (0, 0))],
        out_specs=[small(), small(), small(), pl.BlockSpec((1, n_exp), lambda i: (0, 0))],
        out_shape=[jax.ShapeDtypeStruct((t, TOP_K), I32), jax.ShapeDtypeStruct((t, TOP_K), F32),
                   jax.ShapeDtypeStruct((t, TOP_K), I32), jax.ShapeDtypeStruct((1, n_exp), F32)],
        scratch_shapes=[pltpu.VMEM((1, n_exp), F32)],
        compiler_params=_cparams(1),
        name="moe_router",
    )(x, rw, rb)


def _dispatch_body(pe_ref, pd_ref, dest_ref, x_ref, xs_ref, zbuf, sem, zsem, *, tm, n_exp, n_blocks):
    def zero_block(start):
        return pltpu.make_async_copy(zbuf, xs_ref.at[pl.ds(pl.multiple_of(start, MOE_TM), MOE_TM), :], zsem)

    @pl.when(pl.program_id(0) == 0)
    def _():
        zbuf[...] = jnp.zeros_like(zbuf)
        n_used = pe_ref[n_exp - 1] // MOE_TM
        for e in range(n_exp):
            @pl.when(pd_ref[e] > 0)
            def _():
                zero_block(pe_ref[e] - MOE_TM).start()
        lax.fori_loop(n_used, n_blocks, lambda bi, c: (zero_block(bi * MOE_TM).start(), c)[1], 0)
        for e in range(n_exp):
            @pl.when(pd_ref[e] > 0)
            def _():
                zero_block(pe_ref[e] - MOE_TM).wait()
        lax.fori_loop(n_used, n_blocks, lambda bi, c: (zero_block(bi * MOE_TM).wait(), c)[1], 0)

    def issue(r, _):
        for k in range(TOP_K):
            slot = dest_ref[r * TOP_K + k]
            pltpu.make_async_copy(x_ref.at[pl.ds(r, 1), :], xs_ref.at[pl.ds(slot, 1), :], sem).start(priority=k % 2)
        return 0

    lax.fori_loop(0, tm, issue, 0, unroll=8)
    for _ in range(TOP_K):
        pltpu.make_async_copy(x_ref, xs_ref.at[pl.ds(0, tm), :], sem).wait()


def _dispatch(x, dest_flat, pad_end, padded, n_slots, tm=256):
    t, d = x.shape
    tm = _tile(t, tm)
    n_exp = pad_end.shape[0]
    grid_spec = pltpu.PrefetchScalarGridSpec(
        num_scalar_prefetch=2,
        grid=(t // tm,),
        in_specs=[pl.BlockSpec((tm * TOP_K,), lambda i, pe, pd: (i,), memory_space=pltpu.SMEM),
                  pl.BlockSpec((tm, d), lambda i, pe, pd: (i, 0))],
        out_specs=pl.BlockSpec(memory_space=pl.ANY),
        scratch_shapes=[pltpu.VMEM((MOE_TM, d), F32), pltpu.SemaphoreType.DMA, pltpu.SemaphoreType.DMA],
    )
    return pl.pallas_call(
        functools.partial(_dispatch_body, tm=tm, n_exp=n_exp, n_blocks=n_slots // MOE_TM),
        grid_spec=grid_spec,
        out_shape=jax.ShapeDtypeStruct((n_slots, d), F32),
        compiler_params=_cparams(1),
        name="moe_dispatch",
    )(pad_end, padded, dest_flat, x)


def _experts_body(be_ref, nv_ref, x_ref, wg_ref, wl_ref, bg_ref, bl_ref, wd_ref, bd_ref, o_ref):
    i, j = pl.program_id(0), pl.program_id(1)
    valid = i < nv_ref[0]

    @pl.when(j == 0)
    def _():
        o_ref[...] = jnp.where(valid, jnp.broadcast_to(bd_ref[...], o_ref.shape), 0.0)

    @pl.when(valid)
    def _():
        x = x_ref[...].astype(BF16)
        glu = jnp.dot(x, wg_ref[...], preferred_element_type=F32) + bg_ref[...]
        lin = jnp.dot(x, wl_ref[...], preferred_element_type=F32) + bl_ref[...]
        glu = jnp.minimum(glu, SWIGLU_LIMIT)
        lin = jnp.clip(lin, -SWIGLU_LIMIT, SWIGLU_LIMIT)
        hid = (lin + 1.0) * (glu * jax.nn.sigmoid(SWIGLU_ALPHA * glu))
        o_ref[...] += jnp.dot(hid.astype(BF16), wd_ref[...], preferred_element_type=F32)


def _experts(xs, block_e, n_valid, w_gu, b_gu, w_down, b_down, layer, tf=1024):
    n_slots, d = xs.shape
    f = w_gu.shape[3] // 2
    tf = _tile(f, tf)
    nf = f // tf
    nb = n_slots // MOE_TM

    def exp(i, be, nv):
        return be[jnp.minimum(i, nv[0] - 1)]

    def fch(i, j, nv):
        return jnp.where(i < nv[0], j, nf - 1)

    grid_spec = pltpu.PrefetchScalarGridSpec(
        num_scalar_prefetch=2,
        grid=(nb, nf),
        in_specs=[pl.BlockSpec((MOE_TM, d), lambda i, j, be, nv: (jnp.minimum(i, nv[0] - 1), 0)),
                  pl.BlockSpec((None, None, d, tf), lambda i, j, be, nv: (layer, exp(i, be, nv), 0, fch(i, j, nv))),
                  pl.BlockSpec((None, None, d, tf), lambda i, j, be, nv: (layer, exp(i, be, nv), 0, nf + fch(i, j, nv))),
                  pl.BlockSpec((None, None, 1, tf), lambda i, j, be, nv: (layer, exp(i, be, nv), 0, fch(i, j, nv))),
                  pl.BlockSpec((None, None, 1, tf), lambda i, j, be, nv: (layer, exp(i, be, nv), 0, nf + fch(i, j, nv))),
                  pl.BlockSpec((None, None, tf, d), lambda i, j, be, nv: (layer, exp(i, be, nv), fch(i, j, nv), 0)),
                  pl.BlockSpec((None, None, 1, d), lambda i, j, be, nv: (layer, exp(i, be, nv), 0, 0))],
        out_specs=pl.BlockSpec((MOE_TM, d), lambda i, j, be, nv: (i, 0)),
    )
    return pl.pallas_call(
        _experts_body,
        grid_spec=grid_spec,
        out_shape=jax.ShapeDtypeStruct((n_slots, d), F32),
        compiler_params=_cparams(2),
        name="moe_experts",
    )(block_e, n_valid, xs, w_gu, w_gu, b_gu, b_gu, w_down, b_down)


def _combine_body(dest_ref, y_ref, gt_ref, x_ref, g_ref, b_ref, o_ref, buf, sem, *, tm, alpha):
    def issue(r, _):
        for k in range(TOP_K):
            slot = dest_ref[r * TOP_K + k]
            pltpu.make_async_copy(y_ref.at[pl.ds(slot, 1), :], buf.at[k, pl.ds(r, 1), :], sem).start(priority=k % 2)
        return 0

    lax.fori_loop(0, tm, issue, 0, unroll=8)
    for k in range(TOP_K):
        pltpu.make_async_copy(y_ref.at[pl.ds(0, tm), :], buf.at[k], sem).wait()

    ffn = gt_ref[:, 0:1] * buf[0]
    for k in range(1, TOP_K):
        ffn = ffn + gt_ref[:, k:k + 1] * buf[k]
    o_ref[...] = _layer_norm(alpha * x_ref[...] + ffn, g_ref[...], b_ref[...])


def _combine(ys, dest_flat, gates, x, g, b, *, alpha, tm=256):
    t, d = x.shape
    tm = _tile(t, tm)
    return pl.pallas_call(
        functools.partial(_combine_body, tm=tm, alpha=alpha),
        grid=(t // tm,),
        in_specs=[pl.BlockSpec((tm * TOP_K,), lambda i: (i,), memory_space=pltpu.SMEM),
                  pl.BlockSpec(memory_space=pl.ANY),
                  pl.BlockSpec((tm, TOP_K), lambda i: (i, 0)),
                  pl.BlockSpec((tm, d), lambda i: (i, 0)),
                  pl.BlockSpec((1, d), lambda i: (0, 0)),
                  pl.BlockSpec((1, d), lambda i: (0, 0))],
        out_specs=pl.BlockSpec((tm, d), lambda i: (i, 0)),
        out_shape=jax.ShapeDtypeStruct((t, d), F32),
        scratch_shapes=[pltpu.VMEM((TOP_K, tm, d), F32), pltpu.SemaphoreType.DMA],
        compiler_params=_cparams(1),
        name="moe_combine",
    )(dest_flat, ys, gates, x, g, b)


def _moe(x, rw, rb, w_gu, b_gu, w_down, b_down, layer, g, b, *, alpha):
    t, d = x.shape
    n_exp = rw.shape[1]
    top_e, gates, rank, counts = _router(x, rw, rb[None, :])

    counts = counts[0].astype(I32)
    padded = (counts + MOE_TM - 1) // MOE_TM * MOE_TM
    pad_end = jnp.cumsum(padded)
    pad_start = pad_end - padded
    dest_flat = (pad_start[top_e] + rank).reshape(-1)
    nb = -(-(t * TOP_K + n_exp * (MOE_TM - 1)) // MOE_TM)
    block_start = jnp.arange(nb, dtype=I32) * MOE_TM
    block_e = jnp.minimum(jnp.sum((pad_end[None, :] <= block_start[:, None]).astype(I32), axis=1), n_exp - 1)
    n_valid = pad_end[-1:] // MOE_TM

    xs = _dispatch(x, dest_flat, pad_end, padded, nb * MOE_TM)
    ys = _experts(xs, block_e, n_valid, w_gu, b_gu, w_down, b_down, layer)
    return _combine(ys, dest_flat, gates, x, g, b, alpha=alpha)


def _forget_body(x_ref, wf_ref, bf_ref, c_ref, carry, *, tm):
    @pl.when(pl.program_id(1) == 0)
    def _():
        carry[...] = jnp.zeros_like(carry)

    f = jnp.dot(x_ref[...], wf_ref[...], precision=HIGHEST, preferred_element_type=F32) + bf_ref[...]
    log_f = jnp.minimum(f, 0.0) - jnp.log1p(jnp.exp(-jnp.abs(f)))
    row = lax.broadcasted_iota(I32, (tm, tm), 0)
    col = lax.broadcasted_iota(I32, (tm, tm), 1)
    upto = (col <= row).astype(F32)
    c = jnp.dot(upto, log_f, precision=HIGHEST, preferred_element_type=F32) + carry[...]
    c_ref[...] = c
    carry[...] = c[tm - 1:tm, :]


def _forget_cumsum(x, wf, bf, *, batch, seq, tm=512):
    t, d = x.shape
    nh = wf.shape[1]
    tm = _tile(seq, tm)
    nt = seq // tm
    return pl.pallas_call(
        functools.partial(_forget_body, tm=tm),
        grid=(batch, nt),
        in_specs=[pl.BlockSpec((tm, d), lambda bi, i: (bi * nt + i, 0)),
                  pl.BlockSpec((d, nh), lambda bi, i: (0, 0)),
                  pl.BlockSpec((1, nh), lambda bi, i: (0, 0))],
        out_specs=pl.BlockSpec((tm, nh), lambda bi, i: (bi * nt + i, 0)),
        out_shape=jax.ShapeDtypeStruct((t, nh), F32),
        scratch_shapes=[pltpu.VMEM((1, nh), F32)],
        compiler_params=_cparams(2),
        name="forget_cumsum",
    )(x, wf, bf)


LOG2E = 1.4426950408889634
N_BIAS_PARTS = 3


def _kv_prep_body(k_ref, v_ref, c_ref, ka_ref, va_ref, *, tm, hd, nh):
    lane = lax.broadcasted_iota(I32, (tm, hd), 1)
    ones = jnp.ones((tm, hd), BF16)
    for h in range(nh):
        rest = -LOG2E * c_ref[:, h:h + 1]
        extra = jnp.zeros((tm, hd), F32)
        for part in range(N_BIAS_PARTS):
            piece = rest.astype(BF16).astype(F32)
            extra = jnp.where(lane == part, piece, extra)
            rest = rest - piece
        ka_ref[h] = jnp.concatenate([k_ref[:, h * hd:(h + 1) * hd], extra.astype(BF16)], axis=1)
        va_ref[h] = jnp.concatenate([v_ref[:, h * hd:(h + 1) * hd], ones], axis=1)


def _kv_prep(qkv, c, *, batch, seq, nh, tm=512):
    d = qkv.shape[1] // 3
    hd = d // nh
    tm = _tile(seq, tm)
    nt = seq // tm
    out = jax.ShapeDtypeStruct((batch, nh, seq, 2 * hd), BF16)
    out_spec = pl.BlockSpec((None, nh, tm, 2 * hd), lambda bi, i: (bi, 0, i, 0))
    return pl.pallas_call(
        functools.partial(_kv_prep_body, tm=tm, hd=hd, nh=nh),
        grid=(batch, nt),
        in_specs=[pl.BlockSpec((tm, d), lambda bi, i: (bi * nt + i, 1)),
                  pl.BlockSpec((tm, d), lambda bi, i: (bi * nt + i, 2)),
                  pl.BlockSpec((tm, nh), lambda bi, i: (bi * nt + i, 0))],
        out_specs=[out_spec, out_spec],
        out_shape=[out, out],
        compiler_params=_cparams(2),
        name="fox_kv_prep",
    )(qkv, qkv, c)


def _flash_body(q_ref, ka_ref, va_ref, o_ref, q_scr, m_ref, l_ref, acc_ref, sa_ref, sb_ref, *, tq, tk, hd):
    i = pl.program_id(2)
    ratio = tq // tk
    lane = lax.broadcasted_iota(I32, (tq, hd), 1)
    q_scr[...] = jnp.concatenate([q_ref[...], jnp.where(lane < N_BIAS_PARTS, 1.0, 0.0).astype(BF16)], axis=1)
    m_ref[...] = jnp.full((tq, hd), -jnp.inf, F32)
    l_ref[...] = jnp.zeros((tq, hd), F32)
    acc_ref[...] = jnp.zeros((tq, hd), F32)

    def scores(g, s_ref, masked):
        for r in range(ratio):
            k0 = pl.multiple_of((g * ratio + r) * tk, tk)
            s = lax.dot_general(q_scr[...], ka_ref[pl.ds(k0, tk), :], (((1,), (1,)), ((), ())),
                                preferred_element_type=F32)
            if masked:
                row = lax.broadcasted_iota(I32, (tq, tk), 0)
                col = r * tk + lax.broadcasted_iota(I32, (tq, tk), 1)
                s = jnp.where(col <= row, s, -jnp.inf)
            s_ref[r] = s

    def absorb(g, s_ref):
        m, l, acc = m_ref[...], l_ref[...], acc_ref[...]
        for r in range(ratio):
            k0 = pl.multiple_of((g * ratio + r) * tk, tk)
            chunks = [s_ref[r, :, c * hd:(c + 1) * hd] for c in range(tk // hd)]
            top = functools.reduce(jnp.maximum, chunks)
            m_new = jnp.maximum(m, jnp.max(top, axis=-1, keepdims=True))
            p = jnp.concatenate([jnp.exp2(c - m_new).astype(BF16) for c in chunks], axis=1)
            pv = jnp.dot(p, va_ref[pl.ds(k0, tk), :], preferred_element_type=F32)
            scale = jnp.exp2(m - m_new)
            m, l, acc = m_new, scale * l + pv[:, hd:], scale * acc + pv[:, 0:hd]
        m_ref[...], l_ref[...], acc_ref[...] = m, l, acc

    scores(i, sa_ref, True)

    def stage_pair(p, c):
        absorb(jnp.where(p == 0, i, 2 * p - 1), sa_ref)
        scores(2 * p, sb_ref, False)
        absorb(2 * p, sb_ref)
        scores(2 * p + 1, sa_ref, False)
        return c

    lax.fori_loop(0, i // 2, stage_pair, 0)
    last = i - 1

    @pl.when(i % 2 == 1)
    def _():
        absorb(jnp.where(i == 1, i, last - 1), sa_ref)
        scores(last, sb_ref, False)
        absorb(last, sb_ref)

    @pl.when(i % 2 == 0)
    def _():
        absorb(jnp.where(i == 0, i, last), sa_ref)

    o_ref[...] = (acc_ref[...] / l_ref[...]).astype(o_ref.dtype)


def _flash(qkv, ka, va, *, batch, seq, tq=1024, tk=512):
    t = qkv.shape[0]
    d = qkv.shape[1] // 3
    nh = ka.shape[1]
    hd = d // nh
    tq = _tile(seq, tq)
    tk = _tile(tq, tk)
    nq = seq // tq
    kv_spec = pl.BlockSpec((None, None, seq, 2 * hd), lambda bi, h, i: (bi, h, 0, 0))
    return pl.pallas_call(
        functools.partial(_flash_body, tq=tq, tk=tk, hd=hd),
        grid=(batch, nh, nq),
        in_specs=[pl.BlockSpec((tq, hd), lambda bi, h, i: (bi * nq + i, h)), kv_spec, kv_spec],
        out_specs=pl.BlockSpec((tq, hd), lambda bi, h, i: (bi * nq + i, h)),
        out_shape=jax.ShapeDtypeStruct((t, d), BF16),
        scratch_shapes=[pltpu.VMEM((tq, 2 * hd), BF16), pltpu.VMEM((tq, hd), F32), pltpu.VMEM((tq, hd), F32),
                        pltpu.VMEM((tq, hd), F32), pltpu.VMEM((tq // tk, tq, tk), F32),
                        pltpu.VMEM((tq // tk, tq, tk), F32)],
        compiler_params=_cparams(3),
        name="fox_attention",
    )(qkv, ka, va)


def kernel(x, ab_w_in, ab_pool_w, ab_pool_scale, ab_conv_w, ab_w_out, fox_w_in, fox_b_f, fox_w_o,
           ln1_g, ln1_b, ln2_g, ln2_b, router_w, router_b, w_gu, b_gu, w_down, b_down):
    batch, seq, d = x.shape
    depth = ln1_g.shape[0]
    alpha = (2.0 * depth) ** 0.25
    xt = x.reshape(batch * seq, d)
    w_gu_bf, w_down_bf = w_gu.astype(BF16), w_down.astype(BF16)
    b_gu4, b_down4 = b_gu[:, :, None, :], b_down[:, :, None, :]
    for layer in range(depth):
        j = layer // 2
        g1, b1 = ln1_g[layer][None, :], ln1_b[layer][None, :]
        if layer % 2 == 0:
            w_in = ab_w_in[j].astype(BF16)
            h = _matmul(xt, w_in, jnp.ones((1, w_in.shape[1]), F32), F32)
            xt = _mixer0(h, xt, ab_pool_w[j].astype(BF16), ab_pool_scale[j][None, :], ab_conv_w[j],
                         ab_w_out[j].astype(BF16), g1, b1, seq=seq, alpha=alpha)
        else:
            nh = fox_b_f.shape[1]
            w_in = fox_w_in[j]
            q_scale = jnp.concatenate([jnp.full((1, d), LOG2E * (d // nh) ** -0.5, F32), jnp.ones((1, 2 * d), F32)],
                                      axis=1)
            qkv = _matmul(xt, w_in[:, :3 * d].astype(BF16), q_scale, BF16)
            c = _forget_cumsum(xt, w_in[:, 3 * d:], fox_b_f[j][None, :], batch=batch, seq=seq)
            ka, va = _kv_prep(qkv, c, batch=batch, seq=seq, nh=nh)
            o = _flash(qkv, ka, va, batch=batch, seq=seq)
            xt = _proj_ln(o, fox_w_o[j].astype(BF16), xt, g1, b1, alpha=alpha)
        xt = _moe(xt, router_w[layer], router_b[layer], w_gu_bf, b_gu4, w_down_bf, b_down4, layer,
                  ln2_g[layer][None, :], ln2_b[layer][None, :], alpha=alpha)
    return xt.reshape(batch, seq, d)
```

```python
import functools

import jax
import jax.numpy as jnp
from jax import lax
from jax.experimental import pallas as pl
from jax.experimental.pallas import tpu as pltpu

F32 = jnp.float32
BF16 = jnp.bfloat16
I32 = jnp.int32

LN_EPS = 1e-5
POOL_WINDOWS = (2, 4, 8, 16)
CONV_K = 3
HALO = 16
TOP_K = 4
SWIGLU_ALPHA = 1.702
SWIGLU_LIMIT = 7.0
MOE_TM = 512
VMEM_LIMIT = 56 * 2**20


def _cparams(n_axes):
    return pltpu.CompilerParams(dimension_semantics=("arbitrary",) * n_axes, vmem_limit_bytes=VMEM_LIMIT)


def _dot3(a, b):
    a_hi = a.astype(BF16)
    a_lo = (a - a_hi.astype(F32)).astype(BF16)
    b_hi = b.astype(BF16)
    b_lo = (b - b_hi.astype(F32)).astype(BF16)
    dot = functools.partial(jnp.dot, preferred_element_type=F32)
    return dot(a_hi, b_hi) + (dot(a_lo, b_hi) + dot(a_hi, b_lo))


def _tile(n, pref):
    t = min(n, pref)
    while n % t:
        t -= 128
    assert t > 0, (n, pref)
    return t


def _layer_norm(z, g, b):
    mu = jnp.mean(z, axis=-1, keepdims=True)
    zc = z - mu
    var = jnp.mean(zc * zc, axis=-1, keepdims=True)
    return zc * lax.rsqrt(var + LN_EPS) * g + b


def _mm_body(a_ref, b_ref, s_ref, o_ref, a_bf):
    @pl.when(pl.program_id(1) == 0)
    def _():
        a_bf[...] = a_ref[...].astype(BF16)

    acc = jnp.dot(a_bf[...], b_ref[...], preferred_element_type=F32)
    o_ref[...] = (acc * s_ref[...]).astype(o_ref.dtype)


def _matmul(a, b, col_scale, out_dtype, tm=1024, tn=1024):
    m, k = a.shape
    n = b.shape[1]
    tm, tn = _tile(m, tm), _tile(n, tn)
    return pl.pallas_call(
        _mm_body,
        grid=(m // tm, n // tn),
        in_specs=[pl.BlockSpec((tm, k), lambda i, j: (i, 0)),
                  pl.BlockSpec((k, tn), lambda i, j: (0, j)),
                  pl.BlockSpec((1, tn), lambda i, j: (0, j))],
        out_specs=pl.BlockSpec((tm, tn), lambda i, j: (i, j)),
        out_shape=jax.ShapeDtypeStruct((m, n), out_dtype),
        scratch_shapes=[pltpu.VMEM((tm, k), BF16)],
        compiler_params=_cparams(2),
        name="dense_matmul",
    )(a, b, col_scale)


def _mixer0_body(h_ref, halo_ref, x_ref, pw_ref, ps_ref, cw_ref, wo_ref, g_ref, b_ref, o_ref,
                 *, tm, tiles_per_seq, pool_w, conv_w, alpha):
    t_in_seq = pl.program_id(0) % tiles_per_seq
    first = t_in_seq == 0
    group = pool_w // len(POOL_WINDOWS)
    c0, c1, c2 = pool_w, pool_w + conv_w, pool_w + 2 * conv_w

    a = h_ref[:, 0:pool_w]
    a_halo = jnp.where(first, 0.0, halo_ref[:, 0:pool_w])
    s = jnp.concatenate([a_halo, a], axis=0)
    pos = t_in_seq * tm + lax.broadcasted_iota(I32, (tm, 1), 0)
    pooled = []
    for gi, w in enumerate(POOL_WINDOWS):
        s = s + pltpu.roll(s, w // 2, 0)
        inv_count = 1.0 / jnp.minimum(pos + 1, w).astype(F32)
        p = s[HALO:, 0:group] * inv_count - a[:, gi * group:(gi + 1) * group]
        pooled.append(jnp.dot(p.astype(BF16), pw_ref[gi], preferred_element_type=F32))
        if gi + 1 < len(POOL_WINDOWS):
            s = s[:, group:]
    y_pool = jnp.concatenate(pooled, axis=1) * ps_ref[...]

    z = h_ref[:, c1:c2] * h_ref[:, c2:]
    z_halo = jnp.where(first, 0.0, halo_ref[:, c1:c2] * halo_ref[:, c2:])
    z_ext = jnp.concatenate([z_halo, z], axis=0)
    conv = (cw_ref[2:3, :] * z_ext + cw_ref[1:2, :] * pltpu.roll(z_ext, 1, 0)
            + cw_ref[0:1, :] * pltpu.roll(z_ext, 2, 0))
    y_conv = h_ref[:, c0:c1] * conv[HALO:]

    y = jnp.concatenate([y_pool, y_conv], axis=1).astype(BF16)
    mix = jnp.dot(y, wo_ref[...], preferred_element_type=F32)
    o_ref[...] = _layer_norm(alpha * x_ref[...] + mix, g_ref[...], b_ref[...])


def _mixer0(h, x, pool_w, pool_scale, conv_w, w_out, g, b, *, seq, alpha, tm=256):
    t, d = x.shape
    width = h.shape[1]
    pw = pool_scale.shape[1]
    cw = conv_w.shape[1]
    tm = _tile(seq, tm)
    body = functools.partial(_mixer0_body, tm=tm, tiles_per_seq=seq // tm, pool_w=pw, conv_w=cw, alpha=alpha)
    full = lambda shape: pl.BlockSpec(shape, lambda i: (0,) * len(shape))
    return pl.pallas_call(
        body,
        grid=(t // tm,),
        in_specs=[pl.BlockSpec((tm, width), lambda i: (i, 0)),
                  pl.BlockSpec((HALO, width), lambda i: (jnp.maximum(i * (tm // HALO) - 1, 0), 0)),
                  pl.BlockSpec((tm, d), lambda i: (i, 0)),
                  full(pool_w.shape), full(pool_scale.shape), full(conv_w.shape), full(w_out.shape),
                  full(g.shape), full(b.shape)],
        out_specs=pl.BlockSpec((tm, d), lambda i: (i, 0)),
        out_shape=jax.ShapeDtypeStruct((t, d), F32),
        compiler_params=_cparams(1),
        name="pool_conv_mixer",
    )(h, h, x, pool_w, pool_scale, conv_w, w_out, g, b)


def _proj_ln_body(a_ref, w_ref, x_ref, g_ref, b_ref, o_ref, *, alpha):
    mix = jnp.dot(a_ref[...], w_ref[...], preferred_element_type=F32)
    o_ref[...] = _layer_norm(alpha * x_ref[...] + mix, g_ref[...], b_ref[...])


def _proj_ln(a, w, x, g, b, *, alpha, tm=512):
    t, d = x.shape
    tm = _tile(t, tm)
    full = lambda shape: pl.BlockSpec(shape, lambda i: (0,) * len(shape))
    return pl.pallas_call(
        functools.partial(_proj_ln_body, alpha=alpha),
        grid=(t // tm,),
        in_specs=[pl.BlockSpec((tm, a.shape[1]), lambda i: (i, 0)), full(w.shape),
                  pl.BlockSpec((tm, d), lambda i: (i, 0)), full(g.shape), full(b.shape)],
        out_specs=pl.BlockSpec((tm, d), lambda i: (i, 0)),
        out_shape=jax.ShapeDtypeStruct((t, d), F32),
        compiler_params=_cparams(1),
        name="proj_residual_ln",
    )(a, w, x, g, b)


def _select_columns(cols, tm):
    lane = lax.broadcasted_iota(I32, (tm, len(cols)), 1)
    out = jnp.broadcast_to(cols[-1], (tm, len(cols)))
    for k in range(len(cols) - 2, -1, -1):
        out = jnp.where(lane == k, cols[k], out)
    return out


def _router_body(x_ref, rw_ref, rb_ref, te_ref, gt_ref, rk_ref, cnt_ref, carry, *, tm, n_exp):
    @pl.when(pl.program_id(0) == 0)
    def _():
        carry[...] = jnp.zeros_like(carry)

    logits = _dot3(x_ref[...], rw_ref[...]) + rb_ref[...]
    lane = lax.broadcasted_iota(I32, (tm, n_exp), 1)
    vals, idxs, hots = [], [], []
    rest = logits
    for _ in range(TOP_K):
        m = jnp.max(rest, axis=-1, keepdims=True)
        idx = jnp.min(jnp.where(rest == m, lane, n_exp), axis=-1, keepdims=True)
        hot = lane == idx
        rest = jnp.where(hot, -jnp.inf, rest)
        vals.append(m)
        idxs.append(idx)
        hots.append(hot)
    exps = [jnp.exp(v - vals[0]) for v in vals]
    inv_sum = 1.0 / sum(exps)
    gates = [e * inv_sum for e in exps]

    cnt = sum(h.astype(F32) for h in hots)
    row = lax.broadcasted_iota(I32, (tm, tm), 0)
    col = lax.broadcasted_iota(I32, (tm, tm), 1)
    earlier = (col < row).astype(BF16)
    before = jnp.dot(earlier, cnt.astype(BF16), preferred_element_type=F32) + carry[...]
    ranks = [jnp.sum(jnp.where(h, before, 0.0), axis=-1, keepdims=True).astype(I32) for h in hots]

    te_ref[...] = _select_columns(idxs, tm)
    gt_ref[...] = _select_columns(gates, tm)
    rk_ref[...] = _select_columns(ranks, tm)
    carry[...] = carry[...] + jnp.sum(cnt, axis=0, keepdims=True)
    cnt_ref[...] = carry[...]


def _router(x, rw, rb, tm=512):
    t, d = x.shape
    n_exp = rw.shape[1]
    tm = _tile(t, tm)
    small = lambda: pl.BlockSpec((tm, TOP_K), lambda i: (i, 0))
    return pl.pallas_call(
        functools.partial(_router_body, tm=tm, n_exp=n_exp),
        grid=(t // tm,),
        in_specs=[pl.BlockSpec((tm, d), lambda i: (i, 0)),
                  pl.BlockSpec((d, n_exp), lambda i: (0, 0)),
                  pl.BlockSpec((1, n_exp), lambda i: (0, 0))],
        out_specs=[small(), small(), small(), pl.BlockSpec((1, n_exp), lambda i: (0, 0))],
        out_shape=[jax.ShapeDtypeStruct((t, TOP_K), I32), jax.ShapeDtypeStruct((t, TOP_K), F32),
                   jax.ShapeDtypeStruct((t, TOP_K), I32), jax.ShapeDtypeStruct((1, n_exp), F32)],
        scratch_shapes=[pltpu.VMEM((1, n_exp), F32)],
        compiler_params=_cparams(1),
        name="moe_router",
    )(x, rw, rb)


def _dispatch_body(pe_ref, pd_ref, dest_ref, x_ref, xs_ref, zbuf, sem, zsem, *, tm, n_exp, n_blocks):
    def zero_block(start):
        return pltpu.make_async_copy(zbuf, xs_ref.at[pl.ds(pl.multiple_of(start, MOE_TM), MOE_TM), :], zsem)

    @pl.when(pl.program_id(0) == 0)
    def _():
        zbuf[...] = jnp.zeros_like(zbuf)
        n_used = pe_ref[n_exp - 1] // MOE_TM
        for e in range(n_exp):
            @pl.when(pd_ref[e] > 0)
            def _():
                zero_block(pe_ref[e] - MOE_TM).start()
        lax.fori_loop(n_used, n_blocks, lambda bi, c: (zero_block(bi * MOE_TM).start(), c)[1], 0)
        for e in range(n_exp):
            @pl.when(pd_ref[e] > 0)
            def _():
                zero_block(pe_ref[e] - MOE_TM).wait()
        lax.fori_loop(n_used, n_blocks, lambda bi, c: (zero_block(bi * MOE_TM).wait(), c)[1], 0)

    def issue(r, _):
        for k in range(TOP_K):
            slot = dest_ref[r * TOP_K + k]
            pltpu.make_async_copy(x_ref.at[pl.ds(r, 1), :], xs_ref.at[pl.ds(slot, 1), :], sem).start(priority=k % 2)
        return 0

    lax.fori_loop(0, tm, issue, 0, unroll=8)
    for _ in range(TOP_K):
        pltpu.make_async_copy(x_ref, xs_ref.at[pl.ds(0, tm), :], sem).wait()


def _dispatch(x, dest_flat, pad_end, padded, n_slots, tm=256):
    t, d = x.shape
    tm = _tile(t, tm)
    n_exp = pad_end.shape[0]
    grid_spec = pltpu.PrefetchScalarGridSpec(
        num_scalar_prefetch=2,
        grid=(t // tm,),
        in_specs=[pl.BlockSpec((tm * TOP_K,), lambda i, pe, pd: (i,), memory_space=pltpu.SMEM),
                  pl.BlockSpec((tm, d), lambda i, pe, pd: (i, 0))],
        out_specs=pl.BlockSpec(memory_space=pl.ANY),
        scratch_shapes=[pltpu.VMEM((MOE_TM, d), F32), pltpu.SemaphoreType.DMA, pltpu.SemaphoreType.DMA],
    )
    return pl.pallas_call(
        functools.partial(_dispatch_body, tm=tm, n_exp=n_exp, n_blocks=n_slots // MOE_TM),
        grid_spec=grid_spec,
        out_shape=jax.ShapeDtypeStruct((n_slots, d), F32),
        compiler_params=_cparams(1),
        name="moe_dispatch",
    )(pad_end, padded, dest_flat, x)


def _experts_body(be_ref, nv_ref, x_ref, wg_ref, wl_ref, bg_ref, bl_ref, wd_ref, bd_ref, o_ref):
    i, j = pl.program_id(0), pl.program_id(1)
    valid = i < nv_ref[0]

    @pl.when(j == 0)
    def _():
        o_ref[...] = jnp.where(valid, jnp.broadcast_to(bd_ref[...], o_ref.shape), 0.0)

    @pl.when(valid)
    def _():
        x = x_ref[...].astype(BF16)
        glu = jnp.dot(x, wg_ref[...], preferred_element_type=F32) + bg_ref[...]
        lin = jnp.dot(x, wl_ref[...], preferred_element_type=F32) + bl_ref[...]
        glu = jnp.minimum(glu, SWIGLU_LIMIT)
        lin = jnp.clip(lin, -SWIGLU_LIMIT, SWIGLU_LIMIT)
        hid = (lin + 1.0) * (glu * jax.nn.sigmoid(SWIGLU_ALPHA * glu))
        o_ref[...] += jnp.dot(hid.astype(BF16), wd_ref[...], preferred_element_type=F32)


def _experts(xs, block_e, n_valid, w_gu, b_gu, w_down, b_down, layer, tf=1024):
    n_slots, d = xs.shape
    f = w_gu.shape[3] // 2
    tf = _tile(f, tf)
    nf = f // tf
    nb = n_slots // MOE_TM

    def exp(i, be, nv):
        return be[jnp.minimum(i, nv[0] - 1)]

    def fch(i, j, nv):
        return jnp.where(i < nv[0], j, nf - 1)

    grid_spec = pltpu.PrefetchScalarGridSpec(
        num_scalar_prefetch=2,
        grid=(nb, nf),
        in_specs=[pl.BlockSpec((MOE_TM, d), lambda i, j, be, nv: (jnp.minimum(i, nv[0] - 1), 0)),
                  pl.BlockSpec((None, None, d, tf), lambda i, j, be, nv: (layer, exp(i, be, nv), 0, fch(i, j, nv))),
                  pl.BlockSpec((None, None, d, tf), lambda i, j, be, nv: (layer, exp(i, be, nv), 0, nf + fch(i, j, nv))),
                  pl.BlockSpec((None, None, 1, tf), lambda i, j, be, nv: (layer, exp(i, be, nv), 0, fch(i, j, nv))),
                  pl.BlockSpec((None, None, 1, tf), lambda i, j, be, nv: (layer, exp(i, be, nv), 0, nf + fch(i, j, nv))),
                  pl.BlockSpec((None, None, tf, d), lambda i, j, be, nv: (layer, exp(i, be, nv), fch(i, j, nv), 0)),
                  pl.BlockSpec((None, None, 1, d), lambda i, j, be, nv: (layer, exp(i, be, nv), 0, 0))],
        out_specs=pl.BlockSpec((MOE_TM, d), lambda i, j, be, nv: (i, 0)),
    )
    return pl.pallas_call(
        _experts_body,
        grid_spec=grid_spec,
        out_shape=jax.ShapeDtypeStruct((n_slots, d), F32),
        compiler_params=_cparams(2),
        name="moe_experts",
    )(block_e, n_valid, xs, w_gu, w_gu, b_gu, b_gu, w_down, b_down)


def _combine_body(dest_ref, dest_next_ref, y_ref, gt_ref, x_ref, g_ref, b_ref, o_ref, buf, sems, *, tm, alpha, n_tiles):
    i = pl.program_id(0)
    cur = i % 2

    def gather(idx_ref, which):
        def issue(r, _):
            for k in range(TOP_K):
                slot = idx_ref[r * TOP_K + k]
                pltpu.make_async_copy(y_ref.at[pl.ds(slot, 1), :], buf.at[which, k, pl.ds(r, 1), :],
                                      sems.at[which]).start(priority=k % 2)
            return 0

        lax.fori_loop(0, tm, issue, 0, unroll=8)

    @pl.when(i == 0)
    def _():
        gather(dest_ref, cur)

    @pl.when(i + 1 < n_tiles)
    def _():
        gather(dest_next_ref, 1 - cur)

    for k in range(TOP_K):
        pltpu.make_async_copy(y_ref.at[pl.ds(0, tm), :], buf.at[cur, k], sems.at[cur]).wait()

    ffn = gt_ref[:, 0:1] * buf[cur, 0]
    for k in range(1, TOP_K):
        ffn = ffn + gt_ref[:, k:k + 1] * buf[cur, k]
    o_ref[...] = _layer_norm(alpha * x_ref[...] + ffn, g_ref[...], b_ref[...])


def _combine(ys, dest_flat, gates, x, g, b, *, alpha, tm=256):
    t, d = x.shape
    tm = _tile(t, tm)
    n_tiles = t // tm
    return pl.pallas_call(
        functools.partial(_combine_body, tm=tm, alpha=alpha, n_tiles=n_tiles),
        grid=(n_tiles,),
        in_specs=[pl.BlockSpec((tm * TOP_K,), lambda i: (i,), memory_space=pltpu.SMEM),
                  pl.BlockSpec((tm * TOP_K,), lambda i: (jnp.minimum(i + 1, n_tiles - 1),), memory_space=pltpu.SMEM),
                  pl.BlockSpec(memory_space=pl.ANY),
                  pl.BlockSpec((tm, TOP_K), lambda i: (i, 0)),
                  pl.BlockSpec((tm, d), lambda i: (i, 0)),
                  pl.BlockSpec((1, d), lambda i: (0, 0)),
                  pl.BlockSpec((1, d), lambda i: (0, 0))],
        out_specs=pl.BlockSpec((tm, d), lambda i: (i, 0)),
        out_shape=jax.ShapeDtypeStruct((t, d), F32),
        scratch_shapes=[pltpu.VMEM((2, TOP_K, tm, d), F32), pltpu.SemaphoreType.DMA((2,))],
        compiler_params=_cparams(1),
        name="moe_combine",
    )(dest_flat, dest_flat, ys, gates, x, g, b)


def _moe(x, rw, rb, w_gu, b_gu, w_down, b_down, layer, g, b, *, alpha):
    t, d = x.shape
    n_exp = rw.shape[1]
    top_e, gates, rank, counts = _router(x, rw, rb[None, :])

    counts = counts[0].astype(I32)
    padded = (counts + MOE_TM - 1) // MOE_TM * MOE_TM
    pad_end = jnp.cumsum(padded)
    pad_start = pad_end - padded
    dest_flat = (pad_start[top_e] + rank).reshape(-1)
    nb = -(-(t * TOP_K + n_exp * (MOE_TM - 1)) // MOE_TM)
    block_start = jnp.arange(nb, dtype=I32) * MOE_TM
    block_e = jnp.minimum(jnp.sum((pad_end[None, :] <= block_start[:, None]).astype(I32), axis=1), n_exp - 1)
    n_valid = pad_end[-1:] // MOE_TM

    xs = _dispatch(x, dest_flat, pad_end, padded, nb * MOE_TM)
    ys = _experts(xs, block_e, n_valid, w_gu, b_gu, w_down, b_down, layer)
    return _combine(ys, dest_flat, gates, x, g, b, alpha=alpha)


def _forget_body(x_ref, wf_ref, bf_ref, c_ref, carry, *, tm):
    @pl.when(pl.program_id(1) == 0)
    def _():
        carry[...] = jnp.zeros_like(carry)

    f = _dot3(x_ref[...], wf_ref[...]) + bf_ref[...]
    log_f = jnp.minimum(f, 0.0) - jnp.log1p(jnp.exp(-jnp.abs(f)))
    row = lax.broadcasted_iota(I32, (tm, tm), 0)
    col = lax.broadcasted_iota(I32, (tm, tm), 1)
    upto = (col <= row).astype(F32)
    c = _dot3(upto, log_f) + carry[...]
    c_ref[...] = c
    carry[...] = c[tm - 1:tm, :]


def _forget_cumsum(x, wf, bf, *, batch, seq, tm=512):
    t, d = x.shape
    nh = wf.shape[1]
    tm = _tile(seq, tm)
    nt = seq // tm
    return pl.pallas_call(
        functools.partial(_forget_body, tm=tm),
        grid=(batch, nt),
        in_specs=[pl.BlockSpec((tm, d), lambda bi, i: (bi * nt + i, 0)),
                  pl.BlockSpec((d, nh), lambda bi, i: (0, 0)),
                  pl.BlockSpec((1, nh), lambda bi, i: (0, 0))],
        out_specs=pl.BlockSpec((tm, nh), lambda bi, i: (bi * nt + i, 0)),
        out_shape=jax.ShapeDtypeStruct((t, nh), F32),
        scratch_shapes=[pltpu.VMEM((1, nh), F32)],
        compiler_params=_cparams(2),
        name="forget_cumsum",
    )(x, wf, bf)


LOG2E = 1.4426950408889634
N_BIAS_PARTS = 3


def _kv_prep_body(k_ref, v_ref, c_ref, ka_ref, va_ref, *, tm, hd, nh):
    lane = lax.broadcasted_iota(I32, (tm, hd), 1)
    ones = jnp.ones((tm, hd), BF16)
    for h in range(nh):
        rest = -LOG2E * c_ref[:, h:h + 1]
        extra = jnp.zeros((tm, hd), F32)
        for part in range(N_BIAS_PARTS):
            piece = rest.astype(BF16).astype(F32)
            extra = jnp.where(lane == part, piece, extra)
            rest = rest - piece
        ka_ref[h] = jnp.concatenate([k_ref[:, h * hd:(h + 1) * hd], extra.astype(BF16)], axis=1)
        va_ref[h] = jnp.concatenate([v_ref[:, h * hd:(h + 1) * hd], ones], axis=1)


def _kv_prep(qkv, c, *, batch, seq, nh, tm=512):
    d = qkv.shape[1] // 3
    hd = d // nh
    tm = _tile(seq, tm)
    nt = seq // tm
    out = jax.ShapeDtypeStruct((batch, nh, seq, 2 * hd), BF16)
    out_spec = pl.BlockSpec((None, nh, tm, 2 * hd), lambda bi, i: (bi, 0, i, 0))
    return pl.pallas_call(
        functools.partial(_kv_prep_body, tm=tm, hd=hd, nh=nh),
        grid=(batch, nt),
        in_specs=[pl.BlockSpec((tm, d), lambda bi, i: (bi * nt + i, 1)),
                  pl.BlockSpec((tm, d), lambda bi, i: (bi * nt + i, 2)),
                  pl.BlockSpec((tm, nh), lambda bi, i: (bi * nt + i, 0))],
        out_specs=[out_spec, out_spec],
        out_shape=[out, out],
        compiler_params=_cparams(2),
        name="fox_kv_prep",
    )(qkv, qkv, c)


def _flash_body(q_ref, qn_ref, ka_ref, va_ref, o_ref, q_scr, m_ref, l_ref, acc_ref, sa_ref, sb_ref, *, tq, tk, hd, nq):
    assert tq == 2 * tk
    i = pl.program_id(2)
    lane = lax.broadcasted_iota(I32, (tq, hd), 1)
    ones = jnp.where(lane < N_BIAS_PARTS, 1.0, 0.0).astype(BF16)
    q_scr[...] = jnp.concatenate([q_ref[...], ones], axis=1)
    m_ref[...] = jnp.full((tq, hd), -jnp.inf, F32)
    l_ref[...] = jnp.zeros((tq, hd), F32)
    acc_ref[...] = jnp.zeros((tq, hd), F32)

    def scores(q, j, s_ref, diagonal_half=None):
        k0 = pl.multiple_of(j * tk, tk)
        s = lax.dot_general(q, ka_ref[pl.ds(k0, tk), :], (((1,), (1,)), ((), ())), preferred_element_type=F32)
        if diagonal_half is not None:
            row = lax.broadcasted_iota(I32, (tq, tk), 0)
            col = diagonal_half * tk + lax.broadcasted_iota(I32, (tq, tk), 1)
            s = jnp.where(col <= row, s, -jnp.inf)
        s_ref[...] = s

    def absorb(j, s_ref):
        k0 = pl.multiple_of(j * tk, tk)
        m, l, acc = m_ref[...], l_ref[...], acc_ref[...]
        chunks = [s_ref[:, c * hd:(c + 1) * hd] for c in range(tk // hd)]
        top = functools.reduce(jnp.maximum, chunks)
        m_new = jnp.maximum(m, jnp.max(top, axis=-1, keepdims=True))
        p = jnp.concatenate([jnp.exp2(c - m_new).astype(BF16) for c in chunks], axis=1)
        pv = jnp.dot(p, va_ref[pl.ds(k0, tk), :], preferred_element_type=F32)
        scale = jnp.exp2(m - m_new)
        m_ref[...], l_ref[...], acc_ref[...] = m_new, scale * l + pv[:, hd:], scale * acc + pv[:, 0:hd]

    first = 2 * i

    @pl.when(i == 0)
    def _():
        scores(q_scr[...], first, sa_ref, 0)

    absorb(first, sa_ref)
    scores(q_scr[...], first + 1, sb_ref, 1)

    def stage_pair(p, c):
        absorb(jnp.where(p == 0, first + 1, 2 * p - 1), sb_ref)
        scores(q_scr[...], 2 * p, sa_ref)
        absorb(2 * p, sa_ref)
        scores(q_scr[...], 2 * p + 1, sb_ref)
        return c

    lax.fori_loop(0, i, stage_pair, 0)
    last = jnp.where(i == 0, first + 1, first - 1)

    @pl.when(i + 1 < nq)
    def _():
        absorb(last, sb_ref)
        scores(jnp.concatenate([qn_ref[...], ones], axis=1), first + 2, sa_ref, 0)

    @pl.when(i + 1 == nq)
    def _():
        absorb(last, sb_ref)

    o_ref[...] = (acc_ref[...] / l_ref[...]).astype(o_ref.dtype)


def _flash(qkv, ka, va, *, batch, seq, tq=1024):
    t = qkv.shape[0]
    d = qkv.shape[1] // 3
    nh = ka.shape[1]
    hd = d // nh
    tq = _tile(seq, tq)
    tk = tq // 2
    nq = seq // tq
    kv_spec = pl.BlockSpec((None, None, seq, 2 * hd), lambda bi, h, i: (bi, h, 0, 0))
    return pl.pallas_call(
        functools.partial(_flash_body, tq=tq, tk=tk, hd=hd, nq=nq),
        grid=(batch, nh, nq),
        in_specs=[pl.BlockSpec((tq, hd), lambda bi, h, i: (bi * nq + i, h)),
                  pl.BlockSpec((tq, hd), lambda bi, h, i: (bi * nq + jnp.minimum(i + 1, nq - 1), h)),
                  kv_spec, kv_spec],
        out_specs=pl.BlockSpec((tq, hd), lambda bi, h, i: (bi * nq + i, h)),
        out_shape=jax.ShapeDtypeStruct((t, d), BF16),
        scratch_shapes=[pltpu.VMEM((tq, 2 * hd), BF16), pltpu.VMEM((tq, hd), F32), pltpu.VMEM((tq, hd), F32),
                        pltpu.VMEM((tq, hd), F32), pltpu.VMEM((tq, tk), F32), pltpu.VMEM((tq, tk), F32)],
        compiler_params=_cparams(3),
        name="fox_attention",
    )(qkv, qkv, ka, va)


def kernel(x, ab_w_in, ab_pool_w, ab_pool_scale, ab_conv_w, ab_w_out, fox_w_in, fox_b_f, fox_w_o,
           ln1_g, ln1_b, ln2_g, ln2_b, router_w, router_b, w_gu, b_gu, w_down, b_down):
    batch, seq, d = x.shape
    depth = ln1_g.shape[0]
    alpha = (2.0 * depth) ** 0.25
    xt = x.reshape(batch * seq, d)
    w_gu_bf, w_down_bf = w_gu.astype(BF16), w_down.astype(BF16)
    b_gu4, b_down4 = b_gu[:, :, None, :], b_down[:, :, None, :]
    for layer in range(depth):
        j = layer // 2
        g1, b1 = ln1_g[layer][None, :], ln1_b[layer][None, :]
        if layer % 2 == 0:
            w_in = ab_w_in[j].astype(BF16)
            h = _matmul(xt, w_in, jnp.ones((1, w_in.shape[1]), F32), F32)
            xt = _mixer0(h, xt, ab_pool_w[j].astype(BF16), ab_pool_scale[j][None, :], ab_conv_w[j],
                         ab_w_out[j].astype(BF16), g1, b1, seq=seq, alpha=alpha)
        else:
            nh = fox_b_f.shape[1]
            w_in = fox_w_in[j]
            q_scale = jnp.concatenate([jnp.full((1, d), LOG2E * (d // nh) ** -0.5, F32), jnp.ones((1, 2 * d), F32)],
                                      axis=1)
            qkv = _matmul(xt, w_in[:, :3 * d].astype(BF16), q_scale, BF16)
            c = _forget_cumsum(xt, w_in[:, 3 * d:], fox_b_f[j][None, :], batch=batch, seq=seq)
            ka, va = _kv_prep(qkv, c, batch=batch, seq=seq, nh=nh)
            o = _flash(qkv, ka, va, batch=batch, seq=seq)
            xt = _proj_ln(o, fox_w_o[j].astype(BF16), xt, g1, b1, alpha=alpha)
        xt = _moe(xt, router_w[layer], router_b[layer], w_gu_bf, b_gu4, w_down_bf, b_down4, layer,
                  ln2_g[layer][None, :], ln2_b[layer][None, :], alpha=alpha)
    return xt.reshape(batch, seq, d)
```

```python
import functools

import jax
import jax.numpy as jnp
from jax import lax
from jax.experimental import pallas as pl
from jax.experimental.pallas import tpu as pltpu

F32 = jnp.float32
BF16 = jnp.bfloat16
I32 = jnp.int32

LN_EPS = 1e-5
POOL_WINDOWS = (2, 4, 8, 16)
CONV_K = 3
HALO = 16
TOP_K = 4
SWIGLU_ALPHA = 1.702
SWIGLU_LIMIT = 7.0
MOE_TM = 512
VMEM_LIMIT = 56 * 2**20


def _cparams(n_axes):
    return pltpu.CompilerParams(dimension_semantics=("arbitrary",) * n_axes, vmem_limit_bytes=VMEM_LIMIT)


def _dot3(a, b):
    a_hi = a.astype(BF16)
    a_lo = (a - a_hi.astype(F32)).astype(BF16)
    b_hi = b.astype(BF16)
    b_lo = (b - b_hi.astype(F32)).astype(BF16)
    dot = functools.partial(jnp.dot, preferred_element_type=F32)
    return dot(a_hi, b_hi) + (dot(a_lo, b_hi) + dot(a_hi, b_lo))


def _tile(n, pref):
    t = min(n, pref)
    while n % t:
        t -= 128
    assert t > 0, (n, pref)
    return t


def _layer_norm(z, g, b):
    mu = jnp.mean(z, axis=-1, keepdims=True)
    zc = z - mu
    var = jnp.mean(zc * zc, axis=-1, keepdims=True)
    return zc * lax.rsqrt(var + LN_EPS) * g + b


def _mm_body(a_ref, b_ref, s_ref, o_ref, a_bf):
    @pl.when(pl.program_id(1) == 0)
    def _():
        a_bf[...] = a_ref[...].astype(BF16)

    acc = jnp.dot(a_bf[...], b_ref[...], preferred_element_type=F32)
    o_ref[...] = (acc * s_ref[...]).astype(o_ref.dtype)


def _matmul(a, b, col_scale, out_dtype, tm=1024, tn=1024):
    m, k = a.shape
    n = b.shape[1]
    tm, tn = _tile(m, tm), _tile(n, tn)
    return pl.pallas_call(
        _mm_body,
        grid=(m // tm, n // tn),
        in_specs=[pl.BlockSpec((tm, k), lambda i, j: (i, 0)),
                  pl.BlockSpec((k, tn), lambda i, j: (0, j)),
                  pl.BlockSpec((1, tn), lambda i, j: (0, j))],
        out_specs=pl.BlockSpec((tm, tn), lambda i, j: (i, j)),
        out_shape=jax.ShapeDtypeStruct((m, n), out_dtype),
        scratch_shapes=[pltpu.VMEM((tm, k), BF16)],
        compiler_params=_cparams(2),
        name="dense_matmul",
    )(a, b, col_scale)


def _mixer0_body(h_ref, halo_ref, x_ref, pw_ref, ps_ref, cw_ref, wo_ref, g_ref, b_ref, o_ref,
                 *, tm, tiles_per_seq, pool_w, conv_w, alpha):
    t_in_seq = pl.program_id(0) % tiles_per_seq
    first = t_in_seq == 0
    group = pool_w // len(POOL_WINDOWS)
    c0, c1, c2 = pool_w, pool_w + conv_w, pool_w + 2 * conv_w

    a = h_ref[:, 0:pool_w]
    a_halo = jnp.where(first, 0.0, halo_ref[:, 0:pool_w])
    s = jnp.concatenate([a_halo, a], axis=0)
    pos = t_in_seq * tm + lax.broadcasted_iota(I32, (tm, 1), 0)
    pooled = []
    for gi, w in enumerate(POOL_WINDOWS):
        s = s + pltpu.roll(s, w // 2, 0)
        inv_count = 1.0 / jnp.minimum(pos + 1, w).astype(F32)
        p = s[HALO:, 0:group] * inv_count - a[:, gi * group:(gi + 1) * group]
        pooled.append(jnp.dot(p.astype(BF16), pw_ref[gi], preferred_element_type=F32))
        if gi + 1 < len(POOL_WINDOWS):
            s = s[:, group:]
    y_pool = jnp.concatenate(pooled, axis=1) * ps_ref[...]

    z = h_ref[:, c1:c2] * h_ref[:, c2:]
    z_halo = jnp.where(first, 0.0, halo_ref[:, c1:c2] * halo_ref[:, c2:])
    z_ext = jnp.concatenate([z_halo, z], axis=0)
    conv = (cw_ref[2:3, :] * z_ext + cw_ref[1:2, :] * pltpu.roll(z_ext, 1, 0)
            + cw_ref[0:1, :] * pltpu.roll(z_ext, 2, 0))
    y_conv = h_ref[:, c0:c1] * conv[HALO:]

    y = jnp.concatenate([y_pool, y_conv], axis=1).astype(BF16)
    mix = jnp.dot(y, wo_ref[...], preferred_element_type=F32)
    o_ref[...] = _layer_norm(alpha * x_ref[...] + mix, g_ref[...], b_ref[...])


def _mixer0(h, x, pool_w, pool_scale, conv_w, w_out, g, b, *, seq, alpha, tm=256):
    t, d = x.shape
    width = h.shape[1]
    pw = pool_scale.shape[1]
    cw = conv_w.shape[1]
    tm = _tile(seq, tm)
    body = functools.partial(_mixer0_body, tm=tm, tiles_per_seq=seq // tm, pool_w=pw, conv_w=cw, alpha=alpha)
    full = lambda shape: pl.BlockSpec(shape, lambda i: (0,) * len(shape))
    return pl.pallas_call(
        body,
        grid=(t // tm,),
        in_specs=[pl.BlockSpec((tm, width), lambda i: (i, 0)),
                  pl.BlockSpec((HALO, width), lambda i: (jnp.maximum(i * (tm // HALO) - 1, 0), 0)),
                  pl.BlockSpec((tm, d), lambda i: (i, 0)),
                  full(pool_w.shape), full(pool_scale.shape), full(conv_w.shape), full(w_out.shape),
                  full(g.shape), full(b.shape)],
        out_specs=pl.BlockSpec((tm, d), lambda i: (i, 0)),
        out_shape=jax.ShapeDtypeStruct((t, d), F32),
        compiler_params=_cparams(1),
        name="pool_conv_mixer",
    )(h, h, x, pool_w, pool_scale, conv_w, w_out, g, b)


def _proj_ln_body(a_ref, w_ref, x_ref, g_ref, b_ref, o_ref, *, alpha):
    mix = jnp.dot(a_ref[...], w_ref[...], preferred_element_type=F32)
    o_ref[...] = _layer_norm(alpha * x_ref[...] + mix, g_ref[...], b_ref[...])


def _proj_ln(a, w, x, g, b, *, alpha, tm=512):
    t, d = x.shape
    tm = _tile(t, tm)
    full = lambda shape: pl.BlockSpec(shape, lambda i: (0,) * len(shape))
    return pl.pallas_call(
        functools.partial(_proj_ln_body, alpha=alpha),
        grid=(t // tm,),
        in_specs=[pl.BlockSpec((tm, a.shape[1]), lambda i: (i, 0)), full(w.shape),
                  pl.BlockSpec((tm, d), lambda i: (i, 0)), full(g.shape), full(b.shape)],
        out_specs=pl.BlockSpec((tm, d), lambda i: (i, 0)),
        out_shape=jax.ShapeDtypeStruct((t, d), F32),
        compiler_params=_cparams(1),
        name="proj_residual_ln",
    )(a, w, x, g, b)


def _select_columns(cols, tm):
    lane = lax.broadcasted_iota(I32, (tm, len(cols)), 1)
    out = jnp.broadcast_to(cols[-1], (tm, len(cols)))
    for k in range(len(cols) - 2, -1, -1):
        out = jnp.where(lane == k, cols[k], out)
    return out


def _router_body(x_ref, rw_ref, rb_ref, te_ref, gt_ref, rk_ref, cnt_ref, carry, *, tm, n_exp):
    @pl.when(pl.program_id(0) == 0)
    def _():
        carry[...] = jnp.zeros_like(carry)

    logits = _dot3(x_ref[...], rw_ref[...]) + rb_ref[...]
    lane = lax.broadcasted_iota(I32, (tm, n_exp), 1)
    vals, idxs, hots = [], [], []
    rest = logits
    for _ in range(TOP_K):
        m = jnp.max(rest, axis=-1, keepdims=True)
        idx = jnp.min(jnp.where(rest == m, lane, n_exp), axis=-1, keepdims=True)
        hot = lane == idx
        rest = jnp.where(hot, -jnp.inf, rest)
        vals.append(m)
        idxs.append(idx)
        hots.append(hot)
    exps = [jnp.exp(v - vals[0]) for v in vals]
    inv_sum = 1.0 / sum(exps)
    gates = [e * inv_sum for e in exps]

    cnt = sum(h.astype(F32) for h in hots)
    row = lax.broadcasted_iota(I32, (tm, tm), 0)
    col = lax.broadcasted_iota(I32, (tm, tm), 1)
    earlier = (col < row).astype(BF16)
    before = jnp.dot(earlier, cnt.astype(BF16), preferred_element_type=F32) + carry[...]
    ranks = [jnp.sum(jnp.where(h, before, 0.0), axis=-1, keepdims=True).astype(I32) for h in hots]

    te_ref[...] = _select_columns(idxs, tm)
    gt_ref[...] = _select_columns(gates, tm)
    rk_ref[...] = _select_columns(ranks, tm)
    carry[...] = carry[...] + jnp.sum(cnt, axis=0, keepdims=True)
    cnt_ref[...] = carry[...]


def _router(x, rw, rb, tm=512):
    t, d = x.shape
    n_exp = rw.shape[1]
    tm = _tile(t, tm)
    small = lambda: pl.BlockSpec((tm, TOP_K), lambda i: (i, 0))
    return pl.pallas_call(
        functools.partial(_router_body, tm=tm, n_exp=n_exp),
        grid=(t // tm,),
        in_specs=[pl.BlockSpec((tm, d), lambda i: (i, 0)),
                  pl.BlockSpec((d, n_exp), lambda i: (0, 0)),
                  pl.BlockSpec((1, n_exp), lambda i: (0, 0))],
        out_specs=[small(), small(), small(), pl.BlockSpec((1, n_exp), lambda i: (0, 0))],
        out_shape=[jax.ShapeDtypeStruct((t, TOP_K), I32), jax.ShapeDtypeStruct((t, TOP_K), F32),
                   jax.ShapeDtypeStruct((t, TOP_K), I32), jax.ShapeDtypeStruct((1, n_exp), F32)],
        scratch_shapes=[pltpu.VMEM((1, n_exp), F32)],
        compiler_params=_cparams(1),
        name="moe_router",
    )(x, rw, rb)


def _dispatch_body(pe_ref, pd_ref, dest_ref, x_ref, xs_ref, zbuf, sem, zsem, *, tm, n_exp, n_blocks):
    def zero_block(start):
        return pltpu.make_async_copy(zbuf, xs_ref.at[pl.ds(pl.multiple_of(start, MOE_TM), MOE_TM), :], zsem)

    @pl.when(pl.program_id(0) == 0)
    def _():
        zbuf[...] = jnp.zeros_like(zbuf)
        n_used = pe_ref[n_exp - 1] // MOE_TM
        for e in range(n_exp):
            @pl.when(pd_ref[e] > 0)
            def _():
                zero_block(pe_ref[e] - MOE_TM).start()
        lax.fori_loop(n_used, n_blocks, lambda bi, c: (zero_block(bi * MOE_TM).start(), c)[1], 0)
        for e in range(n_exp):
            @pl.when(pd_ref[e] > 0)
            def _():
                zero_block(pe_ref[e] - MOE_TM).wait()
        lax.fori_loop(n_used, n_blocks, lambda bi, c: (zero_block(bi * MOE_TM).wait(), c)[1], 0)

    def issue(r, _):
        for k in range(TOP_K):
            slot = dest_ref[r * TOP_K + k]
            pltpu.make_async_copy(x_ref.at[pl.ds(r, 1), :], xs_ref.at[pl.ds(slot, 1), :], sem).start(priority=k % 2)
        return 0

    lax.fori_loop(0, tm, issue, 0, unroll=8)
    for _ in range(TOP_K):
        pltpu.make_async_copy(x_ref, xs_ref.at[pl.ds(0, tm), :], sem).wait()


def _dispatch(x, dest_flat, pad_end, padded, n_slots, tm=512):
    t, d = x.shape
    tm = _tile(t, tm)
    n_exp = pad_end.shape[0]
    grid_spec = pltpu.PrefetchScalarGridSpec(
        num_scalar_prefetch=2,
        grid=(t // tm,),
        in_specs=[pl.BlockSpec((tm * TOP_K,), lambda i, pe, pd: (i,), memory_space=pltpu.SMEM),
                  pl.BlockSpec((tm, d), lambda i, pe, pd: (i, 0))],
        out_specs=pl.BlockSpec(memory_space=pl.ANY),
        scratch_shapes=[pltpu.VMEM((MOE_TM, d), F32), pltpu.SemaphoreType.DMA, pltpu.SemaphoreType.DMA],
    )
    return pl.pallas_call(
        functools.partial(_dispatch_body, tm=tm, n_exp=n_exp, n_blocks=n_slots // MOE_TM),
        grid_spec=grid_spec,
        out_shape=jax.ShapeDtypeStruct((n_slots, d), F32),
        compiler_params=_cparams(1),
        name="moe_dispatch",
    )(pad_end, padded, dest_flat, x)


def _experts_body(be_ref, nv_ref, x_ref, wg_ref, wl_ref, bg_ref, bl_ref, wd_ref, bd_ref, o_ref):
    i, j = pl.program_id(0), pl.program_id(1)
    valid = i < nv_ref[0]

    @pl.when(j == 0)
    def _():
        o_ref[...] = jnp.where(valid, jnp.broadcast_to(bd_ref[...], o_ref.shape), 0.0)

    @pl.when(valid)
    def _():
        x = x_ref[...].astype(BF16)
        glu = jnp.dot(x, wg_ref[...], preferred_element_type=F32) + bg_ref[...]
        lin = jnp.dot(x, wl_ref[...], preferred_element_type=F32) + bl_ref[...]
        glu = jnp.minimum(glu, SWIGLU_LIMIT)
        lin = jnp.clip(lin, -SWIGLU_LIMIT, SWIGLU_LIMIT)
        hid = (lin + 1.0) * (glu * jax.nn.sigmoid(SWIGLU_ALPHA * glu))
        o_ref[...] += jnp.dot(hid.astype(BF16), wd_ref[...], preferred_element_type=F32)


def _experts(xs, block_e, n_valid, w_gu, b_gu, w_down, b_down, tf=1024):
    n_slots, d = xs.shape
    f = w_gu.shape[2] // 2
    tf = _tile(f, tf)
    nf = f // tf
    nb = n_slots // MOE_TM

    def exp(i, be, nv):
        return be[jnp.minimum(i, nv[0] - 1)]

    def fch(i, j, nv):
        return jnp.where(i < nv[0], j, nf - 1)

    grid_spec = pltpu.PrefetchScalarGridSpec(
        num_scalar_prefetch=2,
        grid=(nb, nf),
        in_specs=[pl.BlockSpec((MOE_TM, d), lambda i, j, be, nv: (jnp.minimum(i, nv[0] - 1), 0)),
                  pl.BlockSpec((None, d, tf), lambda i, j, be, nv: (exp(i, be, nv), 0, fch(i, j, nv))),
                  pl.BlockSpec((None, d, tf), lambda i, j, be, nv: (exp(i, be, nv), 0, nf + fch(i, j, nv))),
                  pl.BlockSpec((None, 1, tf), lambda i, j, be, nv: (exp(i, be, nv), 0, fch(i, j, nv))),
                  pl.BlockSpec((None, 1, tf), lambda i, j, be, nv: (exp(i, be, nv), 0, nf + fch(i, j, nv))),
                  pl.BlockSpec((None, tf, d), lambda i, j, be, nv: (exp(i, be, nv), fch(i, j, nv), 0)),
                  pl.BlockSpec((None, 1, d), lambda i, j, be, nv: (exp(i, be, nv), 0, 0))],
        out_specs=pl.BlockSpec((MOE_TM, d), lambda i, j, be, nv: (i, 0)),
    )
    return pl.pallas_call(
        _experts_body,
        grid_spec=grid_spec,
        out_shape=jax.ShapeDtypeStruct((n_slots, d), F32),
        compiler_params=_cparams(2),
        name="moe_experts",
    )(block_e, n_valid, xs, w_gu, w_gu, b_gu, b_gu, w_down, b_down)


def _combine_body(dest_ref, dest_next_ref, y_ref, gt_ref, x_ref, g_ref, b_ref, o_ref, buf, sems, *, tm, alpha, n_tiles):
    i = pl.program_id(0)
    cur = i % 2

    def gather(idx_ref, which):
        def issue(r, _):
            for k in range(TOP_K):
                slot = idx_ref[r * TOP_K + k]
                pltpu.make_async_copy(y_ref.at[pl.ds(slot, 1), :], buf.at[which, k, pl.ds(r, 1), :],
                                      sems.at[which]).start(priority=k % 2)
            return 0

        lax.fori_loop(0, tm, issue, 0, unroll=8)

    @pl.when(i == 0)
    def _():
        gather(dest_ref, cur)

    @pl.when(i + 1 < n_tiles)
    def _():
        gather(dest_next_ref, 1 - cur)

    for k in range(TOP_K):
        pltpu.make_async_copy(y_ref.at[pl.ds(0, tm), :], buf.at[cur, k], sems.at[cur]).wait()

    ffn = gt_ref[:, 0:1] * buf[cur, 0]
    for k in range(1, TOP_K):
        ffn = ffn + gt_ref[:, k:k + 1] * buf[cur, k]
    o_ref[...] = _layer_norm(alpha * x_ref[...] + ffn, g_ref[...], b_ref[...])


def _combine(ys, dest_flat, gates, x, g, b, *, alpha, tm=256):
    t, d = x.shape
    tm = _tile(t, tm)
    n_tiles = t // tm
    return pl.pallas_call(
        functools.partial(_combine_body, tm=tm, alpha=alpha, n_tiles=n_tiles),
        grid=(n_tiles,),
        in_specs=[pl.BlockSpec((tm * TOP_K,), lambda i: (i,), memory_space=pltpu.SMEM),
                  pl.BlockSpec((tm * TOP_K,), lambda i: (jnp.minimum(i + 1, n_tiles - 1),), memory_space=pltpu.SMEM),
                  pl.BlockSpec(memory_space=pl.ANY),
                  pl.BlockSpec((tm, TOP_K), lambda i: (i, 0)),
                  pl.BlockSpec((tm, d), lambda i: (i, 0)),
                  pl.BlockSpec((1, d), lambda i: (0, 0)),
                  pl.BlockSpec((1, d), lambda i: (0, 0))],
        out_specs=pl.BlockSpec((tm, d), lambda i: (i, 0)),
        out_shape=jax.ShapeDtypeStruct((t, d), F32),
        scratch_shapes=[pltpu.VMEM((2, TOP_K, tm, d), F32), pltpu.SemaphoreType.DMA((2,))],
        compiler_params=_cparams(1),
        name="moe_combine",
    )(dest_flat, dest_flat, ys, gates, x, g, b)


def _moe(x, rw, rb, w_gu, b_gu, w_down, b_down, g, b, *, alpha):
    t, d = x.shape
    n_exp = rw.shape[1]
    top_e, gates, rank, counts = _router(x, rw, rb[None, :])

    counts = counts[0].astype(I32)
    padded = (counts + MOE_TM - 1) // MOE_TM * MOE_TM
    pad_end = jnp.cumsum(padded)
    pad_start = pad_end - padded
    dest_flat = (pad_start[top_e] + rank).reshape(-1)
    nb = -(-(t * TOP_K + n_exp * (MOE_TM - 1)) // MOE_TM)
    block_start = jnp.arange(nb, dtype=I32) * MOE_TM
    block_e = jnp.minimum(jnp.sum((pad_end[None, :] <= block_start[:, None]).astype(I32), axis=1), n_exp - 1)
    n_valid = pad_end[-1:] // MOE_TM

    xs = _dispatch(x, dest_flat, pad_end, padded, nb * MOE_TM)
    ys = _experts(xs, block_e, n_valid, w_gu, b_gu[:, None, :], w_down, b_down[:, None, :])
    return _combine(ys, dest_flat, gates, x, g, b, alpha=alpha)


def _forget_body(x_ref, wf_ref, bf_ref, c_ref, carry, *, tm):
    @pl.when(pl.program_id(1) == 0)
    def _():
        carry[...] = jnp.zeros_like(carry)

    f = _dot3(x_ref[...], wf_ref[...]) + bf_ref[...]
    log_f = jnp.minimum(f, 0.0) - jnp.log1p(jnp.exp(-jnp.abs(f)))
    row = lax.broadcasted_iota(I32, (tm, tm), 0)
    col = lax.broadcasted_iota(I32, (tm, tm), 1)
    upto = (col <= row).astype(F32)
    c = _dot3(upto, log_f) + carry[...]
    c_ref[...] = c
    carry[...] = c[tm - 1:tm, :]


def _forget_cumsum(x, wf, bf, *, batch, seq, tm=512):
    t, d = x.shape
    nh = wf.shape[1]
    tm = _tile(seq, tm)
    nt = seq // tm
    return pl.pallas_call(
        functools.partial(_forget_body, tm=tm),
        grid=(batch, nt),
        in_specs=[pl.BlockSpec((tm, d), lambda bi, i: (bi * nt + i, 0)),
                  pl.BlockSpec((d, nh), lambda bi, i: (0, 0)),
                  pl.BlockSpec((1, nh), lambda bi, i: (0, 0))],
        out_specs=pl.BlockSpec((tm, nh), lambda bi, i: (bi * nt + i, 0)),
        out_shape=jax.ShapeDtypeStruct((t, nh), F32),
        scratch_shapes=[pltpu.VMEM((1, nh), F32)],
        compiler_params=_cparams(2),
        name="forget_cumsum",
    )(x, wf, bf)


LOG2E = 1.4426950408889634
N_BIAS_PARTS = 3


def _kv_prep_body(k_ref, v_ref, c_ref, ka_ref, va_ref, *, tm, hd, nh):
    lane = lax.broadcasted_iota(I32, (tm, hd), 1)
    ones = jnp.ones((tm, hd), BF16)
    for h in range(nh):
        rest = -LOG2E * c_ref[:, h:h + 1]
        extra = jnp.zeros((tm, hd), F32)
        for part in range(N_BIAS_PARTS):
            piece = rest.astype(BF16).astype(F32)
            extra = jnp.where(lane == part, piece, extra)
            rest = rest - piece
        ka_ref[h] = jnp.concatenate([k_ref[:, h * hd:(h + 1) * hd], extra.astype(BF16)], axis=1)
        va_ref[h] = jnp.concatenate([v_ref[:, h * hd:(h + 1) * hd], ones], axis=1)


def _kv_prep(qkv, c, *, batch, seq, nh, tm=512):
    d = qkv.shape[1] // 3
    hd = d // nh
    tm = _tile(seq, tm)
    nt = seq // tm
    out = jax.ShapeDtypeStruct((batch, nh, seq, 2 * hd), BF16)
    out_spec = pl.BlockSpec((None, nh, tm, 2 * hd), lambda bi, i: (bi, 0, i, 0))
    return pl.pallas_call(
        functools.partial(_kv_prep_body, tm=tm, hd=hd, nh=nh),
        grid=(batch, nt),
        in_specs=[pl.BlockSpec((tm, d), lambda bi, i: (bi * nt + i, 1)),
                  pl.BlockSpec((tm, d), lambda bi, i: (bi * nt + i, 2)),
                  pl.BlockSpec((tm, nh), lambda bi, i: (bi * nt + i, 0))],
        out_specs=[out_spec, out_spec],
        out_shape=[out, out],
        compiler_params=_cparams(2),
        name="fox_kv_prep",
    )(qkv, qkv, c)


def _flash_body(q_ref, qn_ref, ka_ref, va_ref, wa_ref, wb_ref, o_ref, wa_bf_ref, wb_bf_ref,
                q_scr, m_ref, l_ref, acc_ref, sa_ref, sb_ref, *, tq, tk, hd, nq):
    assert tq == 2 * tk
    i = pl.program_id(2)
    lane = lax.broadcasted_iota(I32, (tq, hd), 1)
    ones = jnp.where(lane < N_BIAS_PARTS, 1.0, 0.0).astype(BF16)
    q_scr[...] = jnp.concatenate([q_ref[...], ones], axis=1)
    m_ref[...] = jnp.full((tq, hd), -jnp.inf, F32)
    l_ref[...] = jnp.zeros((tq, hd), F32)
    acc_ref[...] = jnp.zeros((tq, hd), F32)

    def scores(q, j, s_ref, diagonal_half=None):
        k0 = pl.multiple_of(j * tk, tk)
        s = lax.dot_general(q, ka_ref[pl.ds(k0, tk), :], (((1,), (1,)), ((), ())), preferred_element_type=F32)
        if diagonal_half is not None:
            row = lax.broadcasted_iota(I32, (tq, tk), 0)
            col = diagonal_half * tk + lax.broadcasted_iota(I32, (tq, tk), 1)
            s = jnp.where(col <= row, s, -jnp.inf)
        s_ref[...] = s

    def absorb(j, s_ref):
        k0 = pl.multiple_of(j * tk, tk)
        m, l, acc = m_ref[...], l_ref[...], acc_ref[...]
        chunks = [s_ref[:, c * hd:(c + 1) * hd] for c in range(tk // hd)]
        top = functools.reduce(jnp.maximum, chunks)
        m_new = jnp.maximum(m, jnp.max(top, axis=-1, keepdims=True))
        p = jnp.concatenate([jnp.exp2(c - m_new).astype(BF16) for c in chunks], axis=1)
        pv = jnp.dot(p, va_ref[pl.ds(k0, tk), :], preferred_element_type=F32)
        scale = jnp.exp2(m - m_new)
        m_ref[...], l_ref[...], acc_ref[...] = m_new, scale * l + pv[:, hd:], scale * acc + pv[:, 0:hd]

    first = 2 * i

    @pl.when(i == 0)
    def _():
        scores(q_scr[...], first, sa_ref, 0)

    wa_bf_ref[...] = wa_ref[...].astype(BF16)
    wb_bf_ref[...] = wb_ref[...].astype(BF16)
    absorb(first, sa_ref)
    scores(q_scr[...], first + 1, sb_ref, 1)

    def stage_pair(p, c):
        absorb(jnp.where(p == 0, first + 1, 2 * p - 1), sb_ref)
        scores(q_scr[...], 2 * p, sa_ref)
        absorb(2 * p, sa_ref)
        scores(q_scr[...], 2 * p + 1, sb_ref)
        return c

    lax.fori_loop(0, i, stage_pair, 0)
    last = jnp.where(i == 0, first + 1, first - 1)

    @pl.when(i + 1 < nq)
    def _():
        absorb(last, sb_ref)
        scores(jnp.concatenate([qn_ref[...], ones], axis=1), first + 2, sa_ref, 0)

    @pl.when(i + 1 == nq)
    def _():
        absorb(last, sb_ref)

    o_ref[...] = (acc_ref[...] / l_ref[...]).astype(o_ref.dtype)


def _flash(qkv, ka, va, wa, wb, layer, *, batch, seq, tq=1024):
    t = qkv.shape[0]
    d = qkv.shape[1] // 3
    nh = ka.shape[1]
    hd = d // nh
    tq = _tile(seq, tq)
    tk = tq // 2
    nq = seq // tq
    n_steps = batch * nh * nq
    n_layers = wa.shape[0]
    wa2, wb2 = wa.reshape(-1, wa.shape[-1]), wb.reshape(-1, wb.shape[-1])
    ra, rb = wa2.shape[0] // (n_layers * n_steps), wb2.shape[0] // (n_layers * n_steps)
    assert ra * n_layers * n_steps == wa2.shape[0] and rb * n_layers * n_steps == wb2.shape[0]
    step = lambda bi, h, i: ((bi * nh + h) * nq + i, 0)
    step_in = lambda bi, h, i: (layer * n_steps + (bi * nh + h) * nq + i, 0)
    kv_spec = pl.BlockSpec((None, None, seq, 2 * hd), lambda bi, h, i: (bi, h, 0, 0))
    o, wa_bf, wb_bf = pl.pallas_call(
        functools.partial(_flash_body, tq=tq, tk=tk, hd=hd, nq=nq),
        grid=(batch, nh, nq),
        in_specs=[pl.BlockSpec((tq, hd), lambda bi, h, i: (bi * nq + i, h)),
                  pl.BlockSpec((tq, hd), lambda bi, h, i: (bi * nq + jnp.minimum(i + 1, nq - 1), h)),
                  kv_spec, kv_spec,
                  pl.BlockSpec((ra, wa2.shape[1]), step_in), pl.BlockSpec((rb, wb2.shape[1]), step_in)],
        out_specs=[pl.BlockSpec((tq, hd), lambda bi, h, i: (bi * nq + i, h)),
                   pl.BlockSpec((ra, wa2.shape[1]), step), pl.BlockSpec((rb, wb2.shape[1]), step)],
        out_shape=[jax.ShapeDtypeStruct((t, d), BF16),
                   jax.ShapeDtypeStruct((ra * n_steps, wa2.shape[1]), BF16),
                   jax.ShapeDtypeStruct((rb * n_steps, wb2.shape[1]), BF16)],
        scratch_shapes=[pltpu.VMEM((tq, 2 * hd), BF16), pltpu.VMEM((tq, hd), F32), pltpu.VMEM((tq, hd), F32),
                        pltpu.VMEM((tq, hd), F32), pltpu.VMEM((tq, tk), F32), pltpu.VMEM((tq, tk), F32)],
        compiler_params=_cparams(3),
        name="fox_attention",
    )(qkv, qkv, ka, va, wa2, wb2)
    return o, wa_bf.reshape(wa.shape[1:]), wb_bf.reshape(wb.shape[1:])


def kernel(x, ab_w_in, ab_pool_w, ab_pool_scale, ab_conv_w, ab_w_out, fox_w_in, fox_b_f, fox_w_o,
           ln1_g, ln1_b, ln2_g, ln2_b, router_w, router_b, w_gu, b_gu, w_down, b_down):
    batch, seq, d = x.shape
    depth = ln1_g.shape[0]
    alpha = (2.0 * depth) ** 0.25
    xt = x.reshape(batch * seq, d)
    for layer in range(depth):
        j = layer // 2
        g1, b1 = ln1_g[layer][None, :], ln1_b[layer][None, :]
        if layer % 2 == 0:
            w_in = ab_w_in[j].astype(BF16)
            h = _matmul(xt, w_in, jnp.ones((1, w_in.shape[1]), F32), F32)
            xt = _mixer0(h, xt, ab_pool_w[j].astype(BF16), ab_pool_scale[j][None, :], ab_conv_w[j],
                         ab_w_out[j].astype(BF16), g1, b1, seq=seq, alpha=alpha)
            w_gu_bf, w_down_bf = w_gu[layer].astype(BF16), w_down[layer].astype(BF16)
        else:
            nh = fox_b_f.shape[1]
            w_in = fox_w_in[j]
            q_scale = jnp.concatenate([jnp.full((1, d), LOG2E * (d // nh) ** -0.5, F32), jnp.ones((1, 2 * d), F32)],
                                      axis=1)
            qkv = _matmul(xt, w_in[:, :3 * d].astype(BF16), q_scale, BF16)
            c = _forget_cumsum(xt, w_in[:, 3 * d:], fox_b_f[j][None, :], batch=batch, seq=seq)
            ka, va = _kv_prep(qkv, c, batch=batch, seq=seq, nh=nh)
            o, w_gu_bf, w_down_bf = _flash(qkv, ka, va, w_gu, w_down, layer, batch=batch, seq=seq)
            xt = _proj_ln(o, fox_w_o[j].astype(BF16), xt, g1, b1, alpha=alpha)
        xt = _moe(xt, router_w[layer], router_b[layer], w_gu_bf, b_gu[layer], w_down_bf, b_down[layer],
                  ln2_g[layer][None, :], ln2_b[layer][None, :], alpha=alpha)
    return xt.reshape(batch, seq, d)
```

```python
import functools

import jax
import jax.numpy as jnp
from jax import lax
from jax.experimental import pallas as pl
from jax.experimental.pallas import tpu as pltpu

F32 = jnp.float32
BF16 = jnp.bfloat16
I32 = jnp.int32

LN_EPS = 1e-5
POOL_WINDOWS = (2, 4, 8, 16)
CONV_K = 3
HALO = 16
TOP_K = 4
SWIGLU_ALPHA = 1.702
SWIGLU_LIMIT = 7.0
MOE_TM = 512
VMEM_LIMIT = 56 * 2**20


def _cparams(n_axes):
    return pltpu.CompilerParams(dimension_semantics=("arbitrary",) * n_axes, vmem_limit_bytes=VMEM_LIMIT)


def _dot3(a, b):
    a_hi = a.astype(BF16)
    a_lo = (a - a_hi.astype(F32)).astype(BF16)
    b_hi = b.astype(BF16)
    b_lo = (b - b_hi.astype(F32)).astype(BF16)
    dot = functools.partial(jnp.dot, preferred_element_type=F32)
    return dot(a_hi, b_hi) + (dot(a_lo, b_hi) + dot(a_hi, b_lo))


def _tile(n, pref):
    t = min(n, pref)
    while n % t:
        t -= 128
    assert t > 0, (n, pref)
    return t


def _layer_norm(z, g, b):
    mu = jnp.mean(z, axis=-1, keepdims=True)
    zc = z - mu
    var = jnp.mean(zc * zc, axis=-1, keepdims=True)
    return zc * lax.rsqrt(var + LN_EPS) * g + b


def _mm_body(a_ref, b_ref, s_ref, *rest):
    if len(rest) == 4:
        w_ref, o_ref, w_bf_ref, a_bf = rest
        w_bf_ref[...] = w_ref[...].astype(BF16)
    else:
        o_ref, a_bf = rest

    @pl.when(pl.program_id(1) == 0)
    def _():
        a_bf[...] = a_ref[...].astype(BF16)

    acc = jnp.dot(a_bf[...], b_ref[...], preferred_element_type=F32)
    o_ref[...] = (acc * s_ref[...]).astype(o_ref.dtype)


def _matmul(a, b, col_scale, out_dtype, ride=None, tm=1024, tn=1024):
    m, k = a.shape
    n = b.shape[1]
    tm, tn = _tile(m, tm), _tile(n, tn)
    nn = n // tn
    in_specs = [pl.BlockSpec((tm, k), lambda i, j: (i, 0)),
                pl.BlockSpec((k, tn), lambda i, j: (0, j)),
                pl.BlockSpec((1, tn), lambda i, j: (0, j))]
    out_specs = [pl.BlockSpec((tm, tn), lambda i, j: (i, j))]
    out_shape = [jax.ShapeDtypeStruct((m, n), out_dtype)]
    args = [a, b, col_scale]
    if ride is not None:
        w, layer = ride
        n_steps = (m // tm) * nn
        w2 = w.reshape(-1, w.shape[-1])
        rows = w2.shape[0] // (w.shape[0] * n_steps)
        assert rows * w.shape[0] * n_steps == w2.shape[0], (w.shape, n_steps)
        in_specs.append(pl.BlockSpec((rows, w2.shape[1]), lambda i, j: (layer * n_steps + i * nn + j, 0)))
        out_specs.append(pl.BlockSpec((rows, w2.shape[1]), lambda i, j: (i * nn + j, 0)))
        out_shape.append(jax.ShapeDtypeStruct((rows * n_steps, w2.shape[1]), BF16))
        args.append(w2)
    outs = pl.pallas_call(
        _mm_body,
        grid=(m // tm, nn),
        in_specs=in_specs,
        out_specs=out_specs,
        out_shape=out_shape,
        scratch_shapes=[pltpu.VMEM((tm, k), BF16)],
        compiler_params=_cparams(2),
        name="dense_matmul",
    )(*args)
    if ride is None:
        return outs[0]
    return outs[0], outs[1].reshape(ride[0].shape[1:])


def _mixer0_body(h_ref, halo_ref, x_ref, pw_ref, ps_ref, cw_ref, wo_ref, g_ref, b_ref, o_ref,
                 *, tm, tiles_per_seq, pool_w, conv_w, alpha):
    t_in_seq = pl.program_id(0) % tiles_per_seq
    first = t_in_seq == 0
    group = pool_w // len(POOL_WINDOWS)
    c0, c1, c2 = pool_w, pool_w + conv_w, pool_w + 2 * conv_w

    a = h_ref[:, 0:pool_w]
    a_halo = jnp.where(first, 0.0, halo_ref[:, 0:pool_w])
    s = jnp.concatenate([a_halo, a], axis=0)
    pos = t_in_seq * tm + lax.broadcasted_iota(I32, (tm, 1), 0)
    pooled = []
    for gi, w in enumerate(POOL_WINDOWS):
        s = s + pltpu.roll(s, w // 2, 0)
        inv_count = 1.0 / jnp.minimum(pos + 1, w).astype(F32)
        p = s[HALO:, 0:group] * inv_count - a[:, gi * group:(gi + 1) * group]
        pooled.append(jnp.dot(p.astype(BF16), pw_ref[gi], preferred_element_type=F32))
        if gi + 1 < len(POOL_WINDOWS):
            s = s[:, group:]
    y_pool = jnp.concatenate(pooled, axis=1) * ps_ref[...]

    z = h_ref[:, c1:c2] * h_ref[:, c2:]
    z_halo = jnp.where(first, 0.0, halo_ref[:, c1:c2] * halo_ref[:, c2:])
    z_ext = jnp.concatenate([z_halo, z], axis=0)
    conv = (cw_ref[2:3, :] * z_ext + cw_ref[1:2, :] * pltpu.roll(z_ext, 1, 0)
            + cw_ref[0:1, :] * pltpu.roll(z_ext, 2, 0))
    y_conv = h_ref[:, c0:c1] * conv[HALO:]

    y = jnp.concatenate([y_pool, y_conv], axis=1).astype(BF16)
    mix = jnp.dot(y, wo_ref[...], preferred_element_type=F32)
    o_ref[...] = _layer_norm(alpha * x_ref[...] + mix, g_ref[...], b_ref[...])


def _mixer0(h, x, pool_w, pool_scale, conv_w, w_out, g, b, *, seq, alpha, tm=256):
    t, d = x.shape
    width = h.shape[1]
    pw = pool_scale.shape[1]
    cw = conv_w.shape[1]
    tm = _tile(seq, tm)
    body = functools.partial(_mixer0_body, tm=tm, tiles_per_seq=seq // tm, pool_w=pw, conv_w=cw, alpha=alpha)
    full = lambda shape: pl.BlockSpec(shape, lambda i: (0,) * len(shape))
    return pl.pallas_call(
        body,
        grid=(t // tm,),
        in_specs=[pl.BlockSpec((tm, width), lambda i: (i, 0)),
                  pl.BlockSpec((HALO, width), lambda i: (jnp.maximum(i * (tm // HALO) - 1, 0), 0)),
                  pl.BlockSpec((tm, d), lambda i: (i, 0)),
                  full(pool_w.shape), full(pool_scale.shape), full(conv_w.shape), full(w_out.shape),
                  full(g.shape), full(b.shape)],
        out_specs=pl.BlockSpec((tm, d), lambda i: (i, 0)),
        out_shape=jax.ShapeDtypeStruct((t, d), F32),
        compiler_params=_cparams(1),
        name="pool_conv_mixer",
    )(h, h, x, pool_w, pool_scale, conv_w, w_out, g, b)


def _proj_ln_body(a_ref, w_ref, x_ref, g_ref, b_ref, o_ref, *, alpha):
    mix = jnp.dot(a_ref[...], w_ref[...], preferred_element_type=F32)
    o_ref[...] = _layer_norm(alpha * x_ref[...] + mix, g_ref[...], b_ref[...])


def _proj_ln(a, w, x, g, b, *, alpha, tm=512):
    t, d = x.shape
    tm = _tile(t, tm)
    full = lambda shape: pl.BlockSpec(shape, lambda i: (0,) * len(shape))
    return pl.pallas_call(
        functools.partial(_proj_ln_body, alpha=alpha),
        grid=(t // tm,),
        in_specs=[pl.BlockSpec((tm, a.shape[1]), lambda i: (i, 0)), full(w.shape),
                  pl.BlockSpec((tm, d), lambda i: (i, 0)), full(g.shape), full(b.shape)],
        out_specs=pl.BlockSpec((tm, d), lambda i: (i, 0)),
        out_shape=jax.ShapeDtypeStruct((t, d), F32),
        compiler_params=_cparams(1),
        name="proj_residual_ln",
    )(a, w, x, g, b)


def _select_columns(cols, tm):
    lane = lax.broadcasted_iota(I32, (tm, len(cols)), 1)
    out = jnp.broadcast_to(cols[-1], (tm, len(cols)))
    for k in range(len(cols) - 2, -1, -1):
        out = jnp.where(lane == k, cols[k], out)
    return out


def _router_body(x_ref, rw_ref, rb_ref, te_ref, gt_ref, rk_ref, cnt_ref, carry, *, tm, n_exp):
    @pl.when(pl.program_id(0) == 0)
    def _():
        carry[...] = jnp.zeros_like(carry)

    logits = _dot3(x_ref[...], rw_ref[...]) + rb_ref[...]
    lane = lax.broadcasted_iota(I32, (tm, n_exp), 1)
    vals, idxs, hots = [], [], []
    rest = logits
    for _ in range(TOP_K):
        m = jnp.max(rest, axis=-1, keepdims=True)
        idx = jnp.min(jnp.where(rest == m, lane, n_exp), axis=-1, keepdims=True)
        hot = lane == idx
        rest = jnp.where(hot, -jnp.inf, rest)
        vals.append(m)
        idxs.append(idx)
        hots.append(hot)
    exps = [jnp.exp(v - vals[0]) for v in vals]
    inv_sum = 1.0 / sum(exps)
    gates = [e * inv_sum for e in exps]

    cnt = sum(h.astype(F32) for h in hots)
    row = lax.broadcasted_iota(I32, (tm, tm), 0)
    col = lax.broadcasted_iota(I32, (tm, tm), 1)
    earlier = (col < row).astype(BF16)
    before = jnp.dot(earlier, cnt.astype(BF16), preferred_element_type=F32) + carry[...]
    ranks = [jnp.sum(jnp.where(h, before, 0.0), axis=-1, keepdims=True).astype(I32) for h in hots]

    te_ref[...] = _select_columns(idxs, tm)
    gt_ref[...] = _select_columns(gates, tm)
    rk_ref[...] = _select_columns(ranks, tm)
    carry[...] = carry[...] + jnp.sum(cnt, axis=0, keepdims=True)
    cnt_ref[...] = carry[...]


def _router(x, rw, rb, tm=512):
    t, d = x.shape
    n_exp = rw.shape[1]
    tm = _tile(t, tm)
    small = lambda: pl.BlockSpec((tm, TOP_K), lambda i: (i, 0))
    return pl.pallas_call(
        functools.partial(_router_body, tm=tm, n_exp=n_exp),
        grid=(t // tm,),
        in_specs=[pl.BlockSpec((tm, d), lambda i: (i, 0)),
                  pl.BlockSpec((d, n_exp), lambda i: (0, 0)),
                  pl.BlockSpec((1, n_exp), lambda i: (0, 0))],
        out_specs=[small(), small(), small(), pl.BlockSpec((1, n_exp), lambda i: (0, 0))],
        out_shape=[jax.ShapeDtypeStruct((t, TOP_K), I32), jax.ShapeDtypeStruct((t, TOP_K), F32),
                   jax.ShapeDtypeStruct((t, TOP_K), I32), jax.ShapeDtypeStruct((1, n_exp), F32)],
        scratch_shapes=[pltpu.VMEM((1, n_exp), F32)],
        compiler_params=_cparams(1),
        name="moe_router",
    )(x, rw, rb)


def _dispatch_body(pe_ref, pd_ref, dest_ref, x_ref, xs_ref, zbuf, sem, zsem, *, tm, n_exp, n_blocks):
    def zero_block(start):
        return pltpu.make_async_copy(zbuf, xs_ref.at[pl.ds(pl.multiple_of(start, MOE_TM), MOE_TM), :], zsem)

    @pl.when(pl.program_id(0) == 0)
    def _():
        zbuf[...] = jnp.zeros_like(zbuf)
        n_used = pe_ref[n_exp - 1] // MOE_TM
        for e in range(n_exp):
            @pl.when(pd_ref[e] > 0)
            def _():
                zero_block(pe_ref[e] - MOE_TM).start()
        lax.fori_loop(n_used, n_blocks, lambda bi, c: (zero_block(bi * MOE_TM).start(), c)[1], 0)
        for e in range(n_exp):
            @pl.when(pd_ref[e] > 0)
            def _():
                zero_block(pe_ref[e] - MOE_TM).wait()
        lax.fori_loop(n_used, n_blocks, lambda bi, c: (zero_block(bi * MOE_TM).wait(), c)[1], 0)

    def issue(r, _):
        for k in range(TOP_K):
            slot = dest_ref[r * TOP_K + k]
            pltpu.make_async_copy(x_ref.at[pl.ds(r, 1), :], xs_ref.at[pl.ds(slot, 1), :], sem).start(priority=k % 2)
        return 0

    lax.fori_loop(0, tm, issue, 0, unroll=8)
    for _ in range(TOP_K):
        pltpu.make_async_copy(x_ref, xs_ref.at[pl.ds(0, tm), :], sem).wait()


def _dispatch(x, dest_flat, pad_end, padded, n_slots, tm=512):
    t, d = x.shape
    tm = _tile(t, tm)
    n_exp = pad_end.shape[0]
    grid_spec = pltpu.PrefetchScalarGridSpec(
        num_scalar_prefetch=2,
        grid=(t // tm,),
        in_specs=[pl.BlockSpec((tm * TOP_K,), lambda i, pe, pd: (i,), memory_space=pltpu.SMEM),
                  pl.BlockSpec((tm, d), lambda i, pe, pd: (i, 0))],
        out_specs=pl.BlockSpec(memory_space=pl.ANY),
        scratch_shapes=[pltpu.VMEM((MOE_TM, d), F32), pltpu.SemaphoreType.DMA, pltpu.SemaphoreType.DMA],
    )
    return pl.pallas_call(
        functools.partial(_dispatch_body, tm=tm, n_exp=n_exp, n_blocks=n_slots // MOE_TM),
        grid_spec=grid_spec,
        out_shape=jax.ShapeDtypeStruct((n_slots, d), F32),
        compiler_params=_cparams(1),
        name="moe_dispatch",
    )(pad_end, padded, dest_flat, x)


def _experts_body(be_ref, nv_ref, x_ref, wg_ref, wl_ref, bg_ref, bl_ref, wd_ref, bd_ref, o_ref):
    i, j = pl.program_id(0), pl.program_id(1)
    valid = i < nv_ref[0]

    @pl.when(j == 0)
    def _():
        o_ref[...] = jnp.where(valid, jnp.broadcast_to(bd_ref[...], o_ref.shape), 0.0)

    @pl.when(valid)
    def _():
        x = x_ref[...].astype(BF16)
        glu = jnp.dot(x, wg_ref[...], preferred_element_type=F32) + bg_ref[...]
        lin = jnp.dot(x, wl_ref[...], preferred_element_type=F32) + bl_ref[...]
        glu = jnp.minimum(glu, SWIGLU_LIMIT)
        lin = jnp.clip(lin, -SWIGLU_LIMIT, SWIGLU_LIMIT)
        hid = (lin + 1.0) * (glu * jax.nn.sigmoid(SWIGLU_ALPHA * glu))
        o_ref[...] += jnp.dot(hid.astype(BF16), wd_ref[...], preferred_element_type=F32)


def _experts(xs, block_e, n_valid, w_gu, b_gu, w_down, b_down, tf=1024):
    n_slots, d = xs.shape
    f = w_gu.shape[2] // 2
    tf = _tile(f, tf)
    nf = f // tf
    nb = n_slots // MOE_TM

    def exp(i, be, nv):
        return be[jnp.minimum(i, nv[0] - 1)]

    def fch(i, j, nv):
        return jnp.where(i < nv[0], j, nf - 1)

    grid_spec = pltpu.PrefetchScalarGridSpec(
        num_scalar_prefetch=2,
        grid=(nb, nf),
        in_specs=[pl.BlockSpec((MOE_TM, d), lambda i, j, be, nv: (jnp.minimum(i, nv[0] - 1), 0)),
                  pl.BlockSpec((None, d, tf), lambda i, j, be, nv: (exp(i, be, nv), 0, fch(i, j, nv))),
                  pl.BlockSpec((None, d, tf), lambda i, j, be, nv: (exp(i, be, nv), 0, nf + fch(i, j, nv))),
                  pl.BlockSpec((None, 1, tf), lambda i, j, be, nv: (exp(i, be, nv), 0, fch(i, j, nv))),
                  pl.BlockSpec((None, 1, tf), lambda i, j, be, nv: (exp(i, be, nv), 0, nf + fch(i, j, nv))),
                  pl.BlockSpec((None, tf, d), lambda i, j, be, nv: (exp(i, be, nv), fch(i, j, nv), 0)),
                  pl.BlockSpec((None, 1, d), lambda i, j, be, nv: (exp(i, be, nv), 0, 0))],
        out_specs=pl.BlockSpec((MOE_TM, d), lambda i, j, be, nv: (i, 0)),
    )
    return pl.pallas_call(
        _experts_body,
        grid_spec=grid_spec,
        out_shape=jax.ShapeDtypeStruct((n_slots, d), F32),
        compiler_params=_cparams(2),
        name="moe_experts",
    )(block_e, n_valid, xs, w_gu, w_gu, b_gu, b_gu, w_down, b_down)


def _combine_body(dest_ref, dest_next_ref, y_ref, gt_ref, x_ref, g_ref, b_ref, o_ref, buf, sems, *, tm, alpha, n_tiles):
    i = pl.program_id(0)
    cur = i % 2

    def gather(idx_ref, which):
        def issue(r, _):
            for k in range(TOP_K):
                slot = idx_ref[r * TOP_K + k]
                pltpu.make_async_copy(y_ref.at[pl.ds(slot, 1), :], buf.at[which, k, pl.ds(r, 1), :],
                                      sems.at[which]).start(priority=k % 2)
            return 0

        lax.fori_loop(0, tm, issue, 0, unroll=8)

    @pl.when(i == 0)
    def _():
        gather(dest_ref, cur)

    @pl.when(i + 1 < n_tiles)
    def _():
        gather(dest_next_ref, 1 - cur)

    for k in range(TOP_K):
        pltpu.make_async_copy(y_ref.at[pl.ds(0, tm), :], buf.at[cur, k], sems.at[cur]).wait()

    ffn = gt_ref[:, 0:1] * buf[cur, 0]
    for k in range(1, TOP_K):
        ffn = ffn + gt_ref[:, k:k + 1] * buf[cur, k]
    o_ref[...] = _layer_norm(alpha * x_ref[...] + ffn, g_ref[...], b_ref[...])


def _combine(ys, dest_flat, gates, x, g, b, *, alpha, tm=256):
    t, d = x.shape
    tm = _tile(t, tm)
    n_tiles = t // tm
    return pl.pallas_call(
        functools.partial(_combine_body, tm=tm, alpha=alpha, n_tiles=n_tiles),
        grid=(n_tiles,),
        in_specs=[pl.BlockSpec((tm * TOP_K,), lambda i: (i,), memory_space=pltpu.SMEM),
                  pl.BlockSpec((tm * TOP_K,), lambda i: (jnp.minimum(i + 1, n_tiles - 1),), memory_space=pltpu.SMEM),
                  pl.BlockSpec(memory_space=pl.ANY),
                  pl.BlockSpec((tm, TOP_K), lambda i: (i, 0)),
                  pl.BlockSpec((tm, d), lambda i: (i, 0)),
                  pl.BlockSpec((1, d), lambda i: (0, 0)),
                  pl.BlockSpec((1, d), lambda i: (0, 0))],
        out_specs=pl.BlockSpec((tm, d), lambda i: (i, 0)),
        out_shape=jax.ShapeDtypeStruct((t, d), F32),
        scratch_shapes=[pltpu.VMEM((2, TOP_K, tm, d), F32), pltpu.SemaphoreType.DMA((2,))],
        compiler_params=_cparams(1),
        name="moe_combine",
    )(dest_flat, dest_flat, ys, gates, x, g, b)


def _moe(x, rw, rb, w_gu, b_gu, w_down, b_down, g, b, *, alpha):
    t, d = x.shape
    n_exp = rw.shape[1]
    top_e, gates, rank, counts = _router(x, rw, rb[None, :])

    counts = counts[0].astype(I32)
    padded = (counts + MOE_TM - 1) // MOE_TM * MOE_TM
    pad_end = jnp.cumsum(padded)
    pad_start = pad_end - padded
    dest_flat = (pad_start[top_e] + rank).reshape(-1)
    nb = -(-(t * TOP_K + n_exp * (MOE_TM - 1)) // MOE_TM)
    block_start = jnp.arange(nb, dtype=I32) * MOE_TM
    block_e = jnp.minimum(jnp.sum((pad_end[None, :] <= block_start[:, None]).astype(I32), axis=1), n_exp - 1)
    n_valid = pad_end[-1:] // MOE_TM

    xs = _dispatch(x, dest_flat, pad_end, padded, nb * MOE_TM)
    ys = _experts(xs, block_e, n_valid, w_gu, b_gu[:, None, :], w_down, b_down[:, None, :])
    return _combine(ys, dest_flat, gates, x, g, b, alpha=alpha)


def _forget_body(x_ref, wf_ref, bf_ref, c_ref, carry, *, tm):
    @pl.when(pl.program_id(1) == 0)
    def _():
        carry[...] = jnp.zeros_like(carry)

    f = _dot3(x_ref[...], wf_ref[...]) + bf_ref[...]
    log_f = jnp.minimum(f, 0.0) - jnp.log1p(jnp.exp(-jnp.abs(f)))
    row = lax.broadcasted_iota(I32, (tm, tm), 0)
    col = lax.broadcasted_iota(I32, (tm, tm), 1)
    upto = (col <= row).astype(F32)
    c = _dot3(upto, log_f) + carry[...]
    c_ref[...] = c
    carry[...] = c[tm - 1:tm, :]


def _forget_cumsum(x, wf, bf, *, batch, seq, tm=512):
    t, d = x.shape
    nh = wf.shape[1]
    tm = _tile(seq, tm)
    nt = seq // tm
    return pl.pallas_call(
        functools.partial(_forget_body, tm=tm),
        grid=(batch, nt),
        in_specs=[pl.BlockSpec((tm, d), lambda bi, i: (bi * nt + i, 0)),
                  pl.BlockSpec((d, nh), lambda bi, i: (0, 0)),
                  pl.BlockSpec((1, nh), lambda bi, i: (0, 0))],
        out_specs=pl.BlockSpec((tm, nh), lambda bi, i: (bi * nt + i, 0)),
        out_shape=jax.ShapeDtypeStruct((t, nh), F32),
        scratch_shapes=[pltpu.VMEM((1, nh), F32)],
        compiler_params=_cparams(2),
        name="forget_cumsum",
    )(x, wf, bf)


LOG2E = 1.4426950408889634
N_BIAS_PARTS = 3


def _kv_prep_body(k_ref, v_ref, c_ref, ka_ref, va_ref, *, tm, hd, nh):
    lane = lax.broadcasted_iota(I32, (tm, hd), 1)
    ones = jnp.ones((tm, hd), BF16)
    for h in range(nh):
        rest = -LOG2E * c_ref[:, h:h + 1]
        extra = jnp.zeros((tm, hd), F32)
        for part in range(N_BIAS_PARTS):
            piece = rest.astype(BF16).astype(F32)
            extra = jnp.where(lane == part, piece, extra)
            rest = rest - piece
        ka_ref[h] = jnp.concatenate([k_ref[:, h * hd:(h + 1) * hd], extra.astype(BF16)], axis=1)
        va_ref[h] = jnp.concatenate([v_ref[:, h * hd:(h + 1) * hd], ones], axis=1)


def _kv_prep(qkv, c, *, batch, seq, nh, tm=512):
    d = qkv.shape[1] // 3
    hd = d // nh
    tm = _tile(seq, tm)
    nt = seq // tm
    out = jax.ShapeDtypeStruct((batch, nh, seq, 2 * hd), BF16)
    out_spec = pl.BlockSpec((None, nh, tm, 2 * hd), lambda bi, i: (bi, 0, i, 0))
    return pl.pallas_call(
        functools.partial(_kv_prep_body, tm=tm, hd=hd, nh=nh),
        grid=(batch, nt),
        in_specs=[pl.BlockSpec((tm, d), lambda bi, i: (bi * nt + i, 1)),
                  pl.BlockSpec((tm, d), lambda bi, i: (bi * nt + i, 2)),
                  pl.BlockSpec((tm, nh), lambda bi, i: (bi * nt + i, 0))],
        out_specs=[out_spec, out_spec],
        out_shape=[out, out],
        compiler_params=_cparams(2),
        name="fox_kv_prep",
    )(qkv, qkv, c)


def _flash_body(q_ref, qn_ref, ka_ref, va_ref, wa_ref, wb_ref, o_ref, wa_bf_ref, wb_bf_ref,
                q_scr, m_ref, l_ref, acc_ref, sa_ref, sb_ref, *, tq, tk, hd, nq):
    assert tq == 2 * tk
    i = pl.program_id(2)
    lane = lax.broadcasted_iota(I32, (tq, hd), 1)
    ones = jnp.where(lane < N_BIAS_PARTS, 1.0, 0.0).astype(BF16)
    q_scr[...] = jnp.concatenate([q_ref[...], ones], axis=1)
    m_ref[...] = jnp.full((tq, hd), -jnp.inf, F32)
    l_ref[...] = jnp.zeros((tq, hd), F32)
    acc_ref[...] = jnp.zeros((tq, hd), F32)

    def scores(q, j, s_ref, diagonal_half=None):
        k0 = pl.multiple_of(j * tk, tk)
        s = lax.dot_general(q, ka_ref[pl.ds(k0, tk), :], (((1,), (1,)), ((), ())), preferred_element_type=F32)
        if diagonal_half is not None:
            row = lax.broadcasted_iota(I32, (tq, tk), 0)
            col = diagonal_half * tk + lax.broadcasted_iota(I32, (tq, tk), 1)
            s = jnp.where(col <= row, s, -jnp.inf)
        s_ref[...] = s

    def absorb(j, s_ref):
        k0 = pl.multiple_of(j * tk, tk)
        m, l, acc = m_ref[...], l_ref[...], acc_ref[...]
        chunks = [s_ref[:, c * hd:(c + 1) * hd] for c in range(tk // hd)]
        top = functools.reduce(jnp.maximum, chunks)
        m_new = jnp.maximum(m, jnp.max(top, axis=-1, keepdims=True))
        p = jnp.concatenate([jnp.exp2(c - m_new).astype(BF16) for c in chunks], axis=1)
        pv = jnp.dot(p, va_ref[pl.ds(k0, tk), :], preferred_element_type=F32)
        scale = jnp.exp2(m - m_new)
        m_ref[...], l_ref[...], acc_ref[...] = m_new, scale * l + pv[:, hd:], scale * acc + pv[:, 0:hd]

    first = 2 * i

    @pl.when(i == 0)
    def _():
        scores(q_scr[...], first, sa_ref, 0)

    wa_bf_ref[...] = wa_ref[...].astype(BF16)
    wb_bf_ref[...] = wb_ref[...].astype(BF16)
    absorb(first, sa_ref)
    scores(q_scr[...], first + 1, sb_ref, 1)

    def stage_pair(p, c):
        absorb(jnp.where(p == 0, first + 1, 2 * p - 1), sb_ref)
        scores(q_scr[...], 2 * p, sa_ref)
        absorb(2 * p, sa_ref)
        scores(q_scr[...], 2 * p + 1, sb_ref)
        return c

    lax.fori_loop(0, i, stage_pair, 0)
    last = jnp.where(i == 0, first + 1, first - 1)

    @pl.when(i + 1 < nq)
    def _():
        absorb(last, sb_ref)
        scores(jnp.concatenate([qn_ref[...], ones], axis=1), first + 2, sa_ref, 0)

    @pl.when(i + 1 == nq)
    def _():
        absorb(last, sb_ref)

    o_ref[...] = (acc_ref[...] / l_ref[...]).astype(o_ref.dtype)


def _flash(qkv, ka, va, wa, wb, layer, *, batch, seq, tq=1024):
    t = qkv.shape[0]
    d = qkv.shape[1] // 3
    nh = ka.shape[1]
    hd = d // nh
    tq = _tile(seq, tq)
    tk = tq // 2
    nq = seq // tq
    n_steps = batch * nh * nq
    n_layers = wa.shape[0]
    wa2, wb2 = wa.reshape(-1, wa.shape[-1]), wb.reshape(-1, wb.shape[-1])
    ra, rb = wa2.shape[0] // (n_layers * n_steps), wb2.shape[0] // (n_layers * n_steps)
    assert ra * n_layers * n_steps == wa2.shape[0] and rb * n_layers * n_steps == wb2.shape[0]
    step = lambda bi, h, i: ((bi * nh + h) * nq + i, 0)
    step_in = lambda bi, h, i: (layer * n_steps + (bi * nh + h) * nq + i, 0)
    kv_spec = pl.BlockSpec((None, None, seq, 2 * hd), lambda bi, h, i: (bi, h, 0, 0))
    o, wa_bf, wb_bf = pl.pallas_call(
        functools.partial(_flash_body, tq=tq, tk=tk, hd=hd, nq=nq),
        grid=(batch, nh, nq),
        in_specs=[pl.BlockSpec((tq, hd), lambda bi, h, i: (bi * nq + i, h)),
                  pl.BlockSpec((tq, hd), lambda bi, h, i: (bi * nq + jnp.minimum(i + 1, nq - 1), h)),
                  kv_spec, kv_spec,
                  pl.BlockSpec((ra, wa2.shape[1]), step_in), pl.BlockSpec((rb, wb2.shape[1]), step_in)],
        out_specs=[pl.BlockSpec((tq, hd), lambda bi, h, i: (bi * nq + i, h)),
                   pl.BlockSpec((ra, wa2.shape[1]), step), pl.BlockSpec((rb, wb2.shape[1]), step)],
        out_shape=[jax.ShapeDtypeStruct((t, d), BF16),
                   jax.ShapeDtypeStruct((ra * n_steps, wa2.shape[1]), BF16),
                   jax.ShapeDtypeStruct((rb * n_steps, wb2.shape[1]), BF16)],
        scratch_shapes=[pltpu.VMEM((tq, 2 * hd), BF16), pltpu.VMEM((tq, hd), F32), pltpu.VMEM((tq, hd), F32),
                        pltpu.VMEM((tq, hd), F32), pltpu.VMEM((tq, tk), F32), pltpu.VMEM((tq, tk), F32)],
        compiler_params=_cparams(3),
        name="fox_attention",
    )(qkv, qkv, ka, va, wa2, wb2)
    return o, wa_bf.reshape(wa.shape[1:]), wb_bf.reshape(wb.shape[1:])


def kernel(x, ab_w_in, ab_pool_w, ab_pool_scale, ab_conv_w, ab_w_out, fox_w_in, fox_b_f, fox_w_o,
           ln1_g, ln1_b, ln2_g, ln2_b, router_w, router_b, w_gu, b_gu, w_down, b_down):
    batch, seq, d = x.shape
    depth = ln1_g.shape[0]
    alpha = (2.0 * depth) ** 0.25
    xt = x.reshape(batch * seq, d)
    for layer in range(depth):
        j = layer // 2
        g1, b1 = ln1_g[layer][None, :], ln1_b[layer][None, :]
        if layer % 2 == 0:
            w_in = ab_w_in[j].astype(BF16)
            h, w_down_bf = _matmul(xt, w_in, jnp.ones((1, w_in.shape[1]), F32), F32, ride=(w_down, layer))
            xt = _mixer0(h, xt, ab_pool_w[j].astype(BF16), ab_pool_scale[j][None, :], ab_conv_w[j],
                         ab_w_out[j].astype(BF16), g1, b1, seq=seq, alpha=alpha)
            w_gu_bf = w_gu[layer].astype(BF16)
        else:
            nh = fox_b_f.shape[1]
            w_in = fox_w_in[j]
            q_scale = jnp.concatenate([jnp.full((1, d), LOG2E * (d // nh) ** -0.5, F32), jnp.ones((1, 2 * d), F32)],
                                      axis=1)
            qkv = _matmul(xt, w_in[:, :3 * d].astype(BF16), q_scale, BF16)
            c = _forget_cumsum(xt, w_in[:, 3 * d:], fox_b_f[j][None, :], batch=batch, seq=seq)
            ka, va = _kv_prep(qkv, c, batch=batch, seq=seq, nh=nh)
            o, w_gu_bf, w_down_bf = _flash(qkv, ka, va, w_gu, w_down, layer, batch=batch, seq=seq)
            xt = _proj_ln(o, fox_w_o[j].astype(BF16), xt, g1, b1, alpha=alpha)
        xt = _moe(xt, router_w[layer], router_b[layer], w_gu_bf, b_gu[layer], w_down_bf, b_down[layer],
                  ln2_g[layer][None, :], ln2_b[layer][None, :], alpha=alpha)
    return xt.reshape(batch, seq, d)
```
